```python
import math
import jax, jax.numpy as jnp
from jax import lax
import numpy as np

D_MODEL = 1024
BATCH = 32
SEQ = 2048
DEPTH = 1

HEAD_DIM = 64
FOX_HEADS = 8
SWA_Q_HEADS = 8
SWA_KV_HEADS = 2
SWA_GROUP = SWA_Q_HEADS // SWA_KV_HEADS
WINDOW = 128
Q_BLOCK = 128
ROPE_THETA = 10000.0
D_FF = -(-8 * D_MODEL // (3 * 256)) * 256
DEEPNORM_ALPHA = (2 * DEPTH) ** 0.25
DEEPNORM_BETA = (8 * DEPTH) ** -0.25
LN_EPS = 1e-5

FOX_W = FOX_HEADS * HEAD_DIM
SWA_QW = SWA_Q_HEADS * HEAD_DIM
SWA_KVW = SWA_KV_HEADS * HEAD_DIM
IN_COLS = 3 * FOX_W + FOX_HEADS + SWA_QW + 2 * SWA_KVW + 2 * D_MODEL

kernel_name = "fox_swa_sink_gated_hybrid_deepnorm"


def layer_norm(x, g, b):
    xf = x.astype(jnp.float32)
    mu = jnp.mean(xf, axis=-1, keepdims=True)
    var = jnp.mean(jnp.square(xf - mu), axis=-1, keepdims=True)
    y = (xf - mu) * lax.rsqrt(var + LN_EPS)
    return (y * g.astype(jnp.float32) + b.astype(jnp.float32)).astype(x.dtype)


def apply_rope(x, pos):
    half = x.shape[-1] // 2
    inv_freq = ROPE_THETA ** (-jnp.arange(0, half, dtype=jnp.float32) / half)
    ang = pos[:, None] * inv_freq[None, :]
    cos = jnp.cos(ang)[None, :, None, :]
    sin = jnp.sin(ang)[None, :, None, :]
    xf = x.astype(jnp.float32)
    x1, x2 = xf[..., :half], xf[..., half:]
    out = jnp.concatenate([x1 * cos - x2 * sin, x2 * cos + x1 * sin], axis=-1)
    return out.astype(x.dtype)


def forgetting_attention(q, k, v, log_f):
    B, S, H, dh = q.shape
    nb = S // Q_BLOCK
    scale = 1.0 / math.sqrt(dh)
    c = jnp.cumsum(log_f, axis=1)
    cT = jnp.transpose(c, (0, 2, 1))
    qb = jnp.transpose(q.reshape(B, nb, Q_BLOCK, H, dh), (1, 0, 2, 3, 4))
    cb = jnp.transpose(c.reshape(B, nb, Q_BLOCK, H), (1, 0, 3, 2))
    kpos = jnp.arange(S)

    def one_block(args):
        qi, ci, i = args
        s = jnp.einsum('bqhd,bkhd->bhqk', qi, k, preferred_element_type=jnp.float32) * scale
        bias = ci[..., :, None] - cT[:, :, None, :]
        qpos = i * Q_BLOCK + jnp.arange(Q_BLOCK)
        mask = kpos[None, :] <= qpos[:, None]
        logits = jnp.where(mask[None, None], s + bias, -jnp.inf)
        p = jax.nn.softmax(logits, axis=-1)
        return jnp.einsum('bhqk,bkhd->bqhd', p.astype(v.dtype), v)

    out = lax.map(one_block, (qb, cb, jnp.arange(nb)))
    return jnp.transpose(out, (1, 0, 2, 3, 4)).reshape(B, S, H, dh)


def sliding_window_sink_attention(q, k, v, sinks):
    B, S, Hq, dh = q.shape
    W = WINDOW
    nb = S // W
    scale = 1.0 / math.sqrt(dh)
    qb = q.reshape(B, nb, W, SWA_KV_HEADS, SWA_GROUP, dh)

    def banded(t):
        tb = t.reshape(B, nb, W, SWA_KV_HEADS, dh)
        prev = jnp.concatenate([jnp.zeros_like(tb[:, :1]), tb[:, :-1]], axis=1)
        return jnp.concatenate([prev, tb], axis=2)

    kk, vv = banded(k), banded(v)
    s = jnp.einsum('bnqhgd,bnkhd->bnhgqk', qb, kk, preferred_element_type=jnp.float32) * scale
    i = jnp.arange(W)[:, None]
    j = jnp.arange(2 * W)[None, :]
    diff = W + i - j
    key_abs = jnp.arange(nb)[:, None, None] * W + j[None] - W
    mask = (diff >= 0)[None] & (diff < WINDOW)[None] & (key_abs >= 0)
    s = jnp.where(mask[None, :, None, None], s, -jnp.inf)
    sink = sinks.astype(jnp.float32).reshape(SWA_KV_HEADS, SWA_GROUP)[None, None, :, :, None, None]
    m = jnp.maximum(jnp.max(s, axis=-1, keepdims=True), sink)
    e = jnp.exp(s - m)
    denom = jnp.sum(e, axis=-1, keepdims=True) + jnp.exp(sink - m)
    p = e / denom
    out = jnp.einsum('bnhgqk,bnkhd->bnqhgd', p.astype(v.dtype), vv)
    return out.reshape(B, S, Hq, dh)


def hybrid_mixer(x, w_in, b_forget, attn_sinks, w_o_fox, w_o_swa, w_out):
    B, S, _ = x.shape
    proj = jnp.einsum('bsd,dc->bsc', x, w_in)
    sizes = [FOX_W, FOX_W, FOX_W, FOX_HEADS, SWA_QW, SWA_KVW, SWA_KVW, D_MODEL, D_MODEL]
    idx = list(np.cumsum(sizes)[:-1])
    q_f, k_f, v_f, f_logit, q_s, k_s, v_s, g_f, g_s = jnp.split(proj, idx, axis=-1)
    log_f = jax.nn.log_sigmoid((f_logit + b_forget).astype(jnp.float32))
    o_f = forgetting_attention(q_f.reshape(B, S, FOX_HEADS, HEAD_DIM),
                               k_f.reshape(B, S, FOX_HEADS, HEAD_DIM),
                               v_f.reshape(B, S, FOX_HEADS, HEAD_DIM), log_f)
    o_f = jnp.einsum('bsc,cd->bsd', o_f.reshape(B, S, FOX_W), w_o_fox)
    pos = jnp.arange(S, dtype=jnp.float32)
    qs = apply_rope(q_s.reshape(B, S, SWA_Q_HEADS, HEAD_DIM), pos)
    ks = apply_rope(k_s.reshape(B, S, SWA_KV_HEADS, HEAD_DIM), pos)
    o_s = sliding_window_sink_attention(qs, ks, v_s.reshape(B, S, SWA_KV_HEADS, HEAD_DIM), attn_sinks)
    o_s = jnp.einsum('bsc,cd->bsd', o_s.reshape(B, S, SWA_QW), w_o_swa)
    merged = jax.nn.sigmoid(g_f) * o_f + jax.nn.sigmoid(g_s) * o_s
    return jnp.einsum('bsd,de->bse', merged, w_out)


def swiglu_ffn(x, w_gate, w_up, w_down):
    h = jax.nn.silu(jnp.einsum('bsd,df->bsf', x, w_gate)) * jnp.einsum('bsd,df->bsf', x, w_up)
    return jnp.einsum('bsf,fd->bsd', h, w_down)


def setup_inputs(seed: int = 0) -> dict:
    key = jax.random.key(seed)
    ks = jax.random.split(key, 16)
    L = DEPTH
    nrm = jax.random.normal
    x = nrm(ks[0], (BATCH, SEQ, D_MODEL), jnp.float32)
    col_scale = jnp.concatenate([
        jnp.ones((2 * FOX_W,), jnp.float32),
        jnp.full((FOX_W,), DEEPNORM_BETA, jnp.float32),
        jnp.ones((FOX_HEADS + SWA_QW + SWA_KVW,), jnp.float32),
        jnp.full((SWA_KVW,), DEEPNORM_BETA, jnp.float32),
        jnp.ones((2 * D_MODEL,), jnp.float32)])
    w_in = nrm(ks[1], (L, D_MODEL, IN_COLS), jnp.float32) * (D_MODEL ** -0.5) * col_scale
    b_forget = 1.0 + 0.1 * nrm(ks[2], (L, FOX_HEADS), jnp.float32)
    attn_sinks = 0.5 * nrm(ks[3], (L, SWA_Q_HEADS), jnp.float32)
    w_o_fox = nrm(ks[4], (L, FOX_W, D_MODEL), jnp.float32) * (FOX_W ** -0.5)
    w_o_swa = nrm(ks[5], (L, SWA_QW, D_MODEL), jnp.float32) * (SWA_QW ** -0.5)
    w_out = nrm(ks[6], (L, D_MODEL, D_MODEL), jnp.float32) * (D_MODEL ** -0.5) * DEEPNORM_BETA
    ln1_g = 1.0 + 0.02 * nrm(ks[7], (L, D_MODEL), jnp.float32)
    ln1_b = 0.02 * nrm(ks[8], (L, D_MODEL), jnp.float32)
    w_gate = nrm(ks[9], (L, D_MODEL, D_FF), jnp.float32) * (D_MODEL ** -0.5)
    w_up = nrm(ks[10], (L, D_MODEL, D_FF), jnp.float32) * (D_MODEL ** -0.5)
    w_down = nrm(ks[11], (L, D_FF, D_MODEL), jnp.float32) * (D_FF ** -0.5) * DEEPNORM_BETA
    ln2_g = 1.0 + 0.02 * nrm(ks[12], (L, D_MODEL), jnp.float32)
    ln2_b = 0.02 * nrm(ks[13], (L, D_MODEL), jnp.float32)
    return {"x": x, "w_in": w_in, "b_forget": b_forget, "attn_sinks": attn_sinks,
            "w_o_fox": w_o_fox, "w_o_swa": w_o_swa, "w_out": w_out,
            "ln1_g": ln1_g, "ln1_b": ln1_b, "w_gate": w_gate, "w_up": w_up,
            "w_down": w_down, "ln2_g": ln2_g, "ln2_b": ln2_b}


def reference(x, w_in, b_forget, attn_sinks, w_o_fox, w_o_swa, w_out,
              ln1_g, ln1_b, w_gate, w_up, w_down, ln2_g, ln2_b):
    h = x
    for l in range(DEPTH):
        mix = hybrid_mixer(h, w_in[l], b_forget[l], attn_sinks[l], w_o_fox[l], w_o_swa[l], w_out[l])
        h = layer_norm(DEEPNORM_ALPHA * h + mix, ln1_g[l], ln1_b[l])
        ff = swiglu_ffn(h, w_gate[l], w_up[l], w_down[l])
        h = layer_norm(DEEPNORM_ALPHA * h + ff, ln2_g[l], ln2_b[l])
    return h
```

```python
import functools
import math

import numpy as np
import jax
import jax.numpy as jnp
from jax import lax
from jax.experimental import pallas as pl
from jax.experimental.pallas import tpu as pltpu

D_MODEL = 1024
HEAD_DIM = 64
FOX_HEADS = 8
SWA_Q_HEADS = 8
SWA_KV_HEADS = 2
SWA_GROUP = SWA_Q_HEADS // SWA_KV_HEADS
WINDOW = 128
ROPE_THETA = 10000.0
DEPTH = 1
D_FF = -(-8 * D_MODEL // (3 * 256)) * 256
DEEPNORM_ALPHA = (2 * DEPTH) ** 0.25
LN_EPS = 1e-5

FOX_W = FOX_HEADS * HEAD_DIM
SWA_QW = SWA_Q_HEADS * HEAD_DIM
SWA_KVW = SWA_KV_HEADS * HEAD_DIM

LANES = 128
QK_SCALE = 1.0 / math.sqrt(HEAD_DIM)

_C_FOX = 0
_C_SWAQ = _C_FOX + 3 * FOX_W
_C_SWAKV = _C_SWAQ + SWA_QW
_C_GATE = _C_SWAKV + 2 * SWA_KVW
_C_FG = _C_GATE + 2 * D_MODEL
_C_END = _C_FG + LANES

_L_HI, _L_MID, _L_LO, _L_ONE = 0, FOX_HEADS, 2 * FOX_HEADS, 3 * FOX_HEADS

TM_PROJ = 512
TQ_FOX = 256
TS_SWA = 512
TM_OUT = 512
FF_CHUNK = D_FF // 2
VMEM_LIMIT = 56 * 1024 * 1024

_BF16 = jnp.bfloat16
_F32 = jnp.float32


def _dot(a, b):
    return jnp.dot(a, b, preferred_element_type=_F32)


def _dot_nt(a, b):
    return lax.dot_general(a, b, (((1,), (1,)), ((), ())), preferred_element_type=_F32)


def _resident(shape):
    nd = len(shape)
    return pl.BlockSpec(shape, lambda *_: (0,) * nd, pipeline_mode=pl.Buffered(1))


def _inproj_kernel(x_ref, w_ref, cos_ref, sin_ref, bf_ref,
                   fox_ref, swaq_ref, swakv_ref, gate_ref, cs_ref, carry_ref):
    si = pl.program_id(1)
    xb = x_ref[0].astype(_BF16)
    tm = xb.shape[0]

    fox_ref[0] = _dot(xb, w_ref[:, _C_FOX:_C_SWAQ]).astype(_BF16)

    cos = cos_ref[...]
    sin = sin_ref[...]
    lane = lax.broadcasted_iota(jnp.int32, (tm, LANES), 1)
    first_half = (lane & (HEAD_DIM - 1)) < (HEAD_DIM // 2)

    def rope(t):
        partner = jnp.where(first_half,
                            pltpu.roll(t, LANES - HEAD_DIM // 2, 1),
                            pltpu.roll(t, HEAD_DIM // 2, 1))
        return t * cos + partner * sin

    sq = _dot(xb, w_ref[:, _C_SWAQ:_C_SWAKV])
    for c in range(SWA_QW // LANES):
        swaq_ref[0, :, c * LANES:(c + 1) * LANES] = rope(sq[:, c * LANES:(c + 1) * LANES]).astype(_BF16)
    skv = _dot(xb, w_ref[:, _C_SWAKV:_C_GATE])
    swakv_ref[0, :, 0:LANES] = rope(skv[:, 0:LANES]).astype(_BF16)
    swakv_ref[0, :, LANES:2 * LANES] = skv[:, LANES:2 * LANES].astype(_BF16)

    for c in range(4):
        w = 2 * D_MODEL // 4
        g = _dot(xb, w_ref[:, _C_GATE + c * w:_C_GATE + (c + 1) * w])
        gate_ref[0, :, c * w:(c + 1) * w] = (1.0 / (1.0 + jnp.exp(-g))).astype(_BF16)

    f = _dot(xb, w_ref[:, _C_FG:_C_END]) + bf_ref[...]
    logf = jnp.minimum(f, 0.0) - jnp.log1p(jnp.exp(-jnp.abs(f)))
    row = lax.broadcasted_iota(jnp.int32, (tm, LANES), 0)
    c = logf
    k = 1
    while k < tm:
        c = c + jnp.where(row >= k, pltpu.roll(c, k, 0), 0.0)
        k *= 2

    @pl.when(si == 0)
    def _():
        carry_ref[...] = jnp.zeros_like(carry_ref)

    c = c + carry_ref[...]
    carry_ref[...] = c[tm - 1:tm, :]

    hi = c.astype(_BF16)
    r1 = c - hi.astype(_F32)
    mid = r1.astype(_BF16)
    lo = (r1 - mid.astype(_F32)).astype(_BF16)
    one = jnp.where(lane < _L_ONE + 3, 1.0, 0.0).astype(_BF16)
    cs_ref[0] = jnp.where(lane < _L_MID, hi,
                          jnp.where(lane < _L_LO, mid,
                                    jnp.where(lane < _L_ONE, lo, one)))


def _inproj(x, w_all, cos, sin, bf):
    B, S, D = x.shape
    grid = (B, S // TM_PROJ)
    row_blk = lambda w: pl.BlockSpec((1, TM_PROJ, w), lambda b, s: (b, s, 0))
    out_shapes = (
        jax.ShapeDtypeStruct((B, S, 3 * FOX_W), _BF16),
        jax.ShapeDtypeStruct((B, S, SWA_QW), _BF16),
        jax.ShapeDtypeStruct((B, S, 2 * SWA_KVW), _BF16),
        jax.ShapeDtypeStruct((B, S, 2 * D_MODEL), _BF16),
        jax.ShapeDtypeStruct((B, S, LANES), _BF16),
    )
    return pl.pallas_call(
        _inproj_kernel,
        grid=grid,
        in_specs=[
            row_blk(D),
            _resident(w_all.shape),
            pl.BlockSpec((TM_PROJ, LANES), lambda b, s: (s, 0)),
            pl.BlockSpec((TM_PROJ, LANES), lambda b, s: (s, 0)),
            _resident(bf.shape),
        ],
        out_specs=(row_blk(3 * FOX_W), row_blk(SWA_QW), row_blk(2 * SWA_KVW),
                   row_blk(2 * D_MODEL), row_blk(LANES)),
        out_shape=out_shapes,
        scratch_shapes=[pltpu.VMEM((1, LANES), _F32)],
        compiler_params=pltpu.CompilerParams(
            dimension_semantics=("arbitrary", "arbitrary"), vmem_limit_bytes=VMEM_LIMIT),
        name="inproj",
    )(x, w_all, cos, sin, bf)


def _fox_kernel(q_ref, k_ref, v_ref, cs_ref, sel_ref, o_ref, ka_ref):
    hp = pl.program_id(1)
    i = pl.program_id(2)
    tq = q_ref.shape[1]

    @pl.when(i == 0)
    def _():
        cs = cs_ref[0]
        for hh in range(2):
            ka_ref[hh] = _dot(cs, sel_ref[2 * hp + hh]).astype(_BF16)

    q = q_ref[0]
    csq = cs_ref[0, pl.ds(pl.multiple_of(i * tq, tq), tq), :]
    lane = lax.broadcasted_iota(jnp.int32, (tq, LANES), 1)
    causal = (lax.broadcasted_iota(jnp.int32, (tq, tq), 0)
              >= lax.broadcasted_iota(jnp.int32, (tq, tq), 1))

    outs = []
    for hh in range(2):
        in_head = (lane < HEAD_DIM) if hh == 0 else (lane >= HEAD_DIM)
        qa = jnp.concatenate([jnp.where(in_head, q, jnp.zeros_like(q)), csq], axis=1)

        def logits(j):
            rows = pl.ds(pl.multiple_of(j * tq, tq), tq)
            kb = jnp.concatenate([k_ref[0, rows, :], ka_ref[hh, rows, :]], axis=1)
            return _dot_nt(qa, kb), v_ref[0, rows, :]

        s, vb = logits(i)
        s = jnp.where(causal, s, -jnp.inf)
        m = jnp.max(s, axis=1, keepdims=True)
        p = jnp.exp(s - m)
        l = jnp.sum(p, axis=1, keepdims=True)
        acc = _dot(p.astype(_BF16), vb)

        def body(j, carry):
            m, l, acc = carry
            s, vb = logits(j)
            m_new = jnp.maximum(m, jnp.max(s, axis=1, keepdims=True))
            a = jnp.exp(m - m_new)
            p = jnp.exp(s - m_new)
            l = a * l + jnp.sum(p, axis=1, keepdims=True)
            acc = a * acc + _dot(p.astype(_BF16), vb)
            return m_new, l, acc

        m, l, acc = lax.fori_loop(0, i, body, (m, l, acc))
        outs.append(acc / l)

    o_ref[0] = jnp.where(lane < HEAD_DIM, outs[0], outs[1]).astype(_BF16)


def _fox_attention(fox_qkv, cs, sel):
    B, S, _ = fox_qkv.shape
    npair = FOX_HEADS // 2
    grid = (B, npair, S // TQ_FOX)
    return pl.pallas_call(
        _fox_kernel,
        grid=grid,
        in_specs=[
            pl.BlockSpec((1, TQ_FOX, LANES), lambda b, h, i: (b, i, h)),
            pl.BlockSpec((1, S, LANES), lambda b, h, i: (b, 0, npair + h)),
            pl.BlockSpec((1, S, LANES), lambda b, h, i: (b, 0, 2 * npair + h)),
            pl.BlockSpec((1, S, LANES), lambda b, h, i: (b, 0, 0)),
            _resident(sel.shape),
        ],
        out_specs=pl.BlockSpec((1, TQ_FOX, LANES), lambda b, h, i: (b, i, h)),
        out_shape=jax.ShapeDtypeStruct((B, S, FOX_W), _BF16),
        scratch_shapes=[pltpu.VMEM((2, S, LANES), _BF16)],
        compiler_params=pltpu.CompilerParams(
            dimension_semantics=("arbitrary", "arbitrary", "arbitrary"), vmem_limit_bytes=VMEM_LIMIT),
        name="fox_attn",
    )(fox_qkv, fox_qkv, fox_qkv, cs, sel)


def _swa_kernel(sink_ref, q_ref, kv_ref, o_ref):
    i = pl.program_id(1)
    ts = q_ref.shape[1]
    W = WINDOW
    lane = lax.broadcasted_iota(jnp.int32, (W, LANES), 1)
    nrow = SWA_GROUP * W
    qrow = lax.broadcasted_iota(jnp.int32, (nrow, 2 * W), 0) & (W - 1)
    kcol = lax.broadcasted_iota(jnp.int32, (nrow, 2 * W), 1)

    for w in range(ts // W):
        qstart = i * ts + w * W
        kstart = pl.multiple_of(jnp.maximum(qstart - W, 0), W)
        kk = kv_ref[0, pl.ds(kstart, 2 * W), 0:LANES]
        vv = kv_ref[0, pl.ds(kstart, 2 * W), LANES:2 * LANES]
        diff = (qstart + qrow) - (kstart + kcol)
        mask = (diff >= 0) & (diff < WINDOW)

        og = []
        for g in range(SWA_KV_HEADS):
            in_head = (lane < HEAD_DIM) if g == 0 else (lane >= HEAD_DIM)
            rows, sinks = [], []
            for r in range(SWA_GROUP):
                qb = q_ref[0, w * W:(w + 1) * W, r * LANES:(r + 1) * LANES]
                rows.append(jnp.where(in_head, qb, jnp.zeros_like(qb)))
                sinks.append(jnp.full((W, 1), sink_ref[g * SWA_GROUP + r], _F32))
            ql = jnp.concatenate(rows, axis=0)
            sink = jnp.concatenate(sinks, axis=0)
            s = jnp.where(mask, _dot_nt(ql, kk), -jnp.inf)
            m = jnp.maximum(jnp.max(s, axis=1, keepdims=True), sink)
            e = jnp.exp(s - m)
            den = jnp.sum(e, axis=1, keepdims=True) + jnp.exp(sink - m)
            og.append(_dot(e.astype(_BF16), vv) / den)

        for r in range(SWA_GROUP):
            o = jnp.where(lane < HEAD_DIM, og[0][r * W:(r + 1) * W], og[1][r * W:(r + 1) * W])
            o_ref[0, w * W:(w + 1) * W, r * LANES:(r + 1) * LANES] = o.astype(_BF16)


def _swa_attention(sinks, swa_q, swa_kv):
    B, S, _ = swa_q.shape
    grid = (B, S // TS_SWA)
    return pl.pallas_call(
        _swa_kernel,
        grid=grid,
        in_specs=[
            pl.BlockSpec(memory_space=pltpu.SMEM),
            pl.BlockSpec((1, TS_SWA, SWA_QW), lambda b, i: (b, i, 0)),
            pl.BlockSpec((1, S, 2 * SWA_KVW), lambda b, i: (b, 0, 0)),
        ],
        out_specs=pl.BlockSpec((1, TS_SWA, SWA_QW), lambda b, i: (b, i, 0)),
        out_shape=jax.ShapeDtypeStruct((B, S, SWA_QW), _BF16),
        compiler_params=pltpu.CompilerParams(
            dimension_semantics=("arbitrary", "arbitrary"), vmem_limit_bytes=VMEM_LIMIT),
        name="swa_attn",
    )(sinks, swa_q, swa_kv)


def _layer_norm(t, g, b):
    mu = jnp.mean(t, axis=-1, keepdims=True)
    d = t - mu
    var = jnp.mean(d * d, axis=-1, keepdims=True)
    return d * lax.rsqrt(var + LN_EPS) * g + b


def _out_kernel(x_ref, of_ref, os_ref, gate_ref, wof_ref, wos_ref, wout_ref, g1_ref, b1_ref,
                wg_ref, wu_ref, wd_ref, g2_ref, b2_ref, o_ref):
    gf = gate_ref[:, 0:D_MODEL].astype(_F32)
    gs = gate_ref[:, D_MODEL:2 * D_MODEL].astype(_F32)
    merged = gf * _dot(of_ref[...], wof_ref[...]) + gs * _dot(os_ref[...], wos_ref[...])
    mix = _dot(merged.astype(_BF16), wout_ref[...])
    h1 = _layer_norm(DEEPNORM_ALPHA * x_ref[...] + mix, g1_ref[...], b1_ref[...])
    h1b = h1.astype(_BF16)
    ff = None
    for c in range(D_FF // FF_CHUNK):
        cols = slice(c * FF_CHUNK, (c + 1) * FF_CHUNK)
        g = _dot(h1b, wg_ref[:, cols])
        u = _dot(h1b, wu_ref[:, cols])
        hc = (g * (1.0 / (1.0 + jnp.exp(-g))) * u).astype(_BF16)
        part = _dot(hc, wd_ref[cols, :])
        ff = part if ff is None else ff + part
    o_ref[...] = _layer_norm(DEEPNORM_ALPHA * h1 + ff, g2_ref[...], b2_ref[...])


def _out_ffn(x2, o_f, o_s, gates, wof, wos, wout, g1, b1, wg, wu, wd, g2, b2):
    T, D = x2.shape
    grid = (T // TM_OUT,)
    row_blk = lambda w: pl.BlockSpec((TM_OUT, w), lambda t: (t, 0))
    return pl.pallas_call(
        _out_kernel,
        grid=grid,
        in_specs=[row_blk(D), row_blk(FOX_W), row_blk(SWA_QW), row_blk(2 * D_MODEL),
                  _resident(wof.shape), _resident(wos.shape), _resident(wout.shape),
                  _resident(g1.shape), _resident(b1.shape),
                  _resident(wg.shape), _resident(wu.shape), _resident(wd.shape),
                  _resident(g2.shape), _resident(b2.shape)],
        out_specs=row_blk(D),
        out_shape=jax.ShapeDtypeStruct((T, D), _F32),
        compiler_params=pltpu.CompilerParams(
            dimension_semantics=("arbitrary",), vmem_limit_bytes=VMEM_LIMIT),
        name="out_ffn",
    )(x2, o_f, o_s, gates, wof, wos, wout, g1, b1, wg, wu, wd, g2, b2)


def _pair_swa_heads(t, axis):
    shp = t.shape
    t = t.reshape(shp[:axis] + (SWA_KV_HEADS, SWA_GROUP, HEAD_DIM) + shp[axis + 1:])
    return jnp.swapaxes(t, axis, axis + 1).reshape(shp)


def _decay_selectors():
    sel = np.zeros((FOX_HEADS, LANES, LANES), np.float32)
    for h in range(FOX_HEADS):
        for p, base in enumerate((_L_HI, _L_MID, _L_LO)):
            sel[h, _L_ONE + p, base + h] = 1.0
            sel[h, base + h, _L_ONE + p] = -1.0
    return jnp.asarray(sel, _BF16)


def _rope_tables(S):
    half = HEAD_DIM // 2
    inv_freq = ROPE_THETA ** (-jnp.arange(0, half, dtype=_F32) / half)
    ang = jnp.arange(S, dtype=_F32)[:, None] * inv_freq[None, :]
    cos = jnp.tile(jnp.cos(ang), (1, LANES // half))
    sin = jnp.sin(ang)
    sin = jnp.tile(jnp.concatenate([-sin, sin], axis=1), (1, LANES // HEAD_DIM))
    return cos, sin


def kernel(x, w_in, b_forget, attn_sinks, w_o_fox, w_o_swa, w_out, ln1_g, ln1_b,
           w_gate, w_up, w_down, ln2_g, ln2_b):
    B, S, D = x.shape
    assert DEPTH == 1 and w_in.shape[0] == 1
    w = w_in[0]
    sizes = [FOX_W, FOX_W, FOX_W, FOX_HEADS, SWA_QW, SWA_KVW, SWA_KVW, D_MODEL, D_MODEL]
    offs = np.concatenate([[0], np.cumsum(sizes)])
    wq_f, wk_f, wv_f, w_fg, wq_s, wk_s, wv_s, wg_f, wg_s = [w[:, offs[n]:offs[n + 1]] for n in range(9)]

    w_fg3 = jnp.concatenate([w_fg] * 3 + [jnp.zeros((D, LANES - 3 * FOX_HEADS), w.dtype)], axis=1)
    w_all = jnp.concatenate(
        [wq_f * QK_SCALE, wk_f, wv_f, _pair_swa_heads(wq_s, 1) * QK_SCALE, wk_s, wv_s, wg_f, wg_s, w_fg3],
        axis=1).astype(_BF16)
    bf = jnp.concatenate([b_forget[0]] * 3 + [jnp.zeros((LANES - 3 * FOX_HEADS,), _F32)])[None, :]
    cos, sin = _rope_tables(S)

    fox_qkv, swa_q, swa_kv, gates, cs = _inproj(x, w_all, cos, sin, bf)
    o_f = _fox_attention(fox_qkv, cs, _decay_selectors())
    o_s = _swa_attention(attn_sinks[0], swa_q, swa_kv)

    T = B * S
    row = lambda v: v[0][None, :]
    out = _out_ffn(
        x.reshape(T, D), o_f.reshape(T, FOX_W), o_s.reshape(T, SWA_QW), gates.reshape(T, 2 * D_MODEL),
        w_o_fox[0].astype(_BF16), _pair_swa_heads(w_o_swa[0], 0).astype(_BF16), w_out[0].astype(_BF16),
        row(ln1_g), row(ln1_b),
        w_gate[0].astype(_BF16), w_up[0].astype(_BF16), w_down[0].astype(_BF16),
        row(ln2_g), row(ln2_b))
    return out.reshape(B, S, D)
```

```python
import functools
import math

import numpy as np
import jax
import jax.numpy as jnp
from jax import lax
from jax.experimental import pallas as pl
from jax.experimental.pallas import tpu as pltpu

D_MODEL = 1024
HEAD_DIM = 64
FOX_HEADS = 8
SWA_Q_HEADS = 8
SWA_KV_HEADS = 2
SWA_GROUP = SWA_Q_HEADS // SWA_KV_HEADS
WINDOW = 128
ROPE_THETA = 10000.0
DEPTH = 1
D_FF = -(-8 * D_MODEL // (3 * 256)) * 256
DEEPNORM_ALPHA = (2 * DEPTH) ** 0.25
LN_EPS = 1e-5

FOX_W = FOX_HEADS * HEAD_DIM
SWA_QW = SWA_Q_HEADS * HEAD_DIM
SWA_KVW = SWA_KV_HEADS * HEAD_DIM

LANES = 128
QK_SCALE = 1.0 / math.sqrt(HEAD_DIM)

_C_FOX = 0
_C_SWAQ = _C_FOX + 3 * FOX_W
_C_SWAKV = _C_SWAQ + SWA_QW
_C_GATE = _C_SWAKV + 2 * SWA_KVW
_C_FG = _C_GATE + 2 * D_MODEL
_C_END = _C_FG + LANES

_L_HI, _L_MID, _L_LO, _L_ONE = 0, FOX_HEADS, 2 * FOX_HEADS, 3 * FOX_HEADS

TM_PROJ = 512
TQ_FOX = 256
TS_SWA = 512
TM_OUT = 512
FF_CHUNK = D_FF // 2
VMEM_LIMIT = 56 * 1024 * 1024

_BF16 = jnp.bfloat16
_F32 = jnp.float32


def _dot(a, b):
    return jnp.dot(a, b, preferred_element_type=_F32)


def _dot_nt(a, b):
    return lax.dot_general(a, b, (((1,), (1,)), ((), ())), preferred_element_type=_F32)


def _resident(shape):
    nd = len(shape)
    return pl.BlockSpec(shape, lambda *_: (0,) * nd, pipeline_mode=pl.Buffered(1))


def _inproj_kernel(x_ref, w_ref, cos_ref, sin_ref, bf_ref,
                   fox_ref, swaq_ref, swakv_ref, gate_ref, cs_ref, carry_ref):
    si = pl.program_id(1)
    xb = x_ref[0].astype(_BF16)
    tm = xb.shape[0]

    fox_ref[0] = _dot(xb, w_ref[:, _C_FOX:_C_SWAQ]).astype(_BF16)

    cos = cos_ref[...]
    sin = sin_ref[...]
    lane = lax.broadcasted_iota(jnp.int32, (tm, LANES), 1)
    first_half = (lane & (HEAD_DIM - 1)) < (HEAD_DIM // 2)

    def rope(t):
        partner = jnp.where(first_half,
                            pltpu.roll(t, LANES - HEAD_DIM // 2, 1),
                            pltpu.roll(t, HEAD_DIM // 2, 1))
        return t * cos + partner * sin

    sq = _dot(xb, w_ref[:, _C_SWAQ:_C_SWAKV])
    for c in range(SWA_QW // LANES):
        swaq_ref[0, :, c * LANES:(c + 1) * LANES] = rope(sq[:, c * LANES:(c + 1) * LANES]).astype(_BF16)
    skv = _dot(xb, w_ref[:, _C_SWAKV:_C_GATE])
    swakv_ref[0, :, 0:LANES] = rope(skv[:, 0:LANES]).astype(_BF16)
    swakv_ref[0, :, LANES:2 * LANES] = skv[:, LANES:2 * LANES].astype(_BF16)

    for c in range(4):
        w = 2 * D_MODEL // 4
        g = _dot(xb, w_ref[:, _C_GATE + c * w:_C_GATE + (c + 1) * w])
        gate_ref[0, :, c * w:(c + 1) * w] = (1.0 / (1.0 + jnp.exp(-g))).astype(_BF16)

    f = _dot(xb, w_ref[:, _C_FG:_C_END]) + bf_ref[...]
    logf = jnp.minimum(f, 0.0) - jnp.log1p(jnp.exp(-jnp.abs(f)))
    row = lax.broadcasted_iota(jnp.int32, (tm, LANES), 0)
    c = logf
    k = 1
    while k < tm:
        c = c + jnp.where(row >= k, pltpu.roll(c, k, 0), 0.0)
        k *= 2

    @pl.when(si == 0)
    def _():
        carry_ref[...] = jnp.zeros_like(carry_ref)

    c = c + carry_ref[...]
    carry_ref[...] = c[tm - 1:tm, :]

    hi = c.astype(_BF16)
    r1 = c - hi.astype(_F32)
    mid = r1.astype(_BF16)
    lo = (r1 - mid.astype(_F32)).astype(_BF16)
    one = jnp.where(lane < _L_ONE + 6, 1.0, 0.0).astype(_BF16)
    cs_ref[0] = jnp.where(lane < _L_MID, hi,
                          jnp.where(lane < _L_LO, mid,
                                    jnp.where(lane < _L_ONE, lo, one)))


def _inproj(x, w_all, cos, sin, bf):
    B, S, D = x.shape
    grid = (B, S // TM_PROJ)
    row_blk = lambda w: pl.BlockSpec((1, TM_PROJ, w), lambda b, s: (b, s, 0))
    out_shapes = (
        jax.ShapeDtypeStruct((B, S, 3 * FOX_W), _BF16),
        jax.ShapeDtypeStruct((B, S, SWA_QW), _BF16),
        jax.ShapeDtypeStruct((B, S, 2 * SWA_KVW), _BF16),
        jax.ShapeDtypeStruct((B, S, 2 * D_MODEL), _BF16),
        jax.ShapeDtypeStruct((B, S, LANES), _BF16),
    )
    return pl.pallas_call(
        _inproj_kernel,
        grid=grid,
        in_specs=[
            row_blk(D),
            _resident(w_all.shape),
            pl.BlockSpec((TM_PROJ, LANES), lambda b, s: (s, 0)),
            pl.BlockSpec((TM_PROJ, LANES), lambda b, s: (s, 0)),
            _resident(bf.shape),
        ],
        out_specs=(row_blk(3 * FOX_W), row_blk(SWA_QW), row_blk(2 * SWA_KVW),
                   row_blk(2 * D_MODEL), row_blk(LANES)),
        out_shape=out_shapes,
        scratch_shapes=[pltpu.VMEM((1, LANES), _F32)],
        compiler_params=pltpu.CompilerParams(
            dimension_semantics=("arbitrary", "arbitrary"), vmem_limit_bytes=VMEM_LIMIT),
        name="inproj",
    )(x, w_all, cos, sin, bf)


def _fox_kernel(q_ref, k_ref, v_ref, cs_ref, sel_ref, o_ref):
    hp = pl.program_id(1)
    S = q_ref.shape[1]
    tq = TQ_FOX
    h0 = 2 * hp

    cs = cs_ref[0]
    ka = _dot(cs, sel_ref[hp]).astype(_BF16)
    kb = jnp.concatenate([k_ref[0], ka], axis=1)
    v = v_ref[0]

    lane = lax.broadcasted_iota(jnp.int32, (tq, LANES), 1)
    piece = lane & (FOX_HEADS - 1)
    is_piece = lane < _L_ONE
    ones_of = jnp.where(lane < _L_ONE + 3, 0, jnp.where(lane < _L_ONE + 6, 1, -1))
    keep0 = jnp.where(is_piece, piece - h0, ones_of) == 0
    keep1 = jnp.where(is_piece, piece - h0, ones_of) == 1
    causal = ((lax.broadcasted_iota(jnp.int32, (2 * tq, tq), 0) & (tq - 1))
              >= lax.broadcasted_iota(jnp.int32, (2 * tq, tq), 1))

    for i in range(S // tq):
        rows = slice(i * tq, (i + 1) * tq)
        w = (i + 1) * tq
        q = q_ref[0, rows, :]
        csq = cs[rows, :]
        zq = jnp.zeros_like(q)
        qa = jnp.concatenate([
            jnp.concatenate([jnp.where(lane < HEAD_DIM, q, zq), jnp.where(keep0, csq, zq)], axis=1),
            jnp.concatenate([jnp.where(lane >= HEAD_DIM, q, zq), jnp.where(keep1, csq, zq)], axis=1),
        ], axis=0)
        s = _dot_nt(qa, kb[0:w, :])
        sd = jnp.where(causal, s[:, w - tq:w], -jnp.inf)
        m = jnp.max(sd, axis=1, keepdims=True)
        if i > 0:
            so = s[:, 0:w - tq]
            m = jnp.maximum(m, jnp.max(so, axis=1, keepdims=True))
            p = jnp.concatenate([jnp.exp(so - m), jnp.exp(sd - m)], axis=1)
        else:
            p = jnp.exp(sd - m)
        l = jnp.sum(p, axis=1, keepdims=True)
        o = _dot(p.astype(_BF16), v[0:w, :]) / l
        o_ref[0, rows, :] = jnp.where(lane < HEAD_DIM, o[0:tq], o[tq:2 * tq]).astype(_BF16)


def _fox_attention(fox_qkv, cs, sel):
    B, S, _ = fox_qkv.shape
    npair = FOX_HEADS // 2
    grid = (B, npair)
    return pl.pallas_call(
        _fox_kernel,
        grid=grid,
        in_specs=[
            pl.BlockSpec((1, S, LANES), lambda b, h: (b, 0, h)),
            pl.BlockSpec((1, S, LANES), lambda b, h: (b, 0, npair + h)),
            pl.BlockSpec((1, S, LANES), lambda b, h: (b, 0, 2 * npair + h)),
            pl.BlockSpec((1, S, LANES), lambda b, h: (b, 0, 0)),
            _resident(sel.shape),
        ],
        out_specs=pl.BlockSpec((1, S, LANES), lambda b, h: (b, 0, h)),
        out_shape=jax.ShapeDtypeStruct((B, S, FOX_W), _BF16),
        compiler_params=pltpu.CompilerParams(
            dimension_semantics=("arbitrary", "arbitrary"), vmem_limit_bytes=VMEM_LIMIT),
        name="fox_attn",
    )(fox_qkv, fox_qkv, fox_qkv, cs, sel)


def _swa_kernel(sink_ref, q_ref, kv_ref, o_ref):
    i = pl.program_id(1)
    ts = q_ref.shape[1]
    W = WINDOW
    lane = lax.broadcasted_iota(jnp.int32, (W, LANES), 1)
    nrow = SWA_GROUP * W
    qrow = lax.broadcasted_iota(jnp.int32, (nrow, 2 * W), 0) & (W - 1)
    kcol = lax.broadcasted_iota(jnp.int32, (nrow, 2 * W), 1)

    for w in range(ts // W):
        qstart = i * ts + w * W
        kstart = pl.multiple_of(jnp.maximum(qstart - W, 0), W)
        kk = kv_ref[0, pl.ds(kstart, 2 * W), 0:LANES]
        vv = kv_ref[0, pl.ds(kstart, 2 * W), LANES:2 * LANES]
        diff = (qstart + qrow) - (kstart + kcol)
        mask = (diff >= 0) & (diff < WINDOW)

        og = []
        for g in range(SWA_KV_HEADS):
            in_head = (lane < HEAD_DIM) if g == 0 else (lane >= HEAD_DIM)
            rows, sinks = [], []
            for r in range(SWA_GROUP):
                qb = q_ref[0, w * W:(w + 1) * W, r * LANES:(r + 1) * LANES]
                rows.append(jnp.where(in_head, qb, jnp.zeros_like(qb)))
                sinks.append(jnp.full((W, 1), sink_ref[g * SWA_GROUP + r], _F32))
            ql = jnp.concatenate(rows, axis=0)
            sink = jnp.concatenate(sinks, axis=0)
            s = jnp.where(mask, _dot_nt(ql, kk), -jnp.inf)
            m = jnp.maximum(jnp.max(s, axis=1, keepdims=True), sink)
            e = jnp.exp(s - m)
            den = jnp.sum(e, axis=1, keepdims=True) + jnp.exp(sink - m)
            og.append(_dot(e.astype(_BF16), vv) / den)

        for r in range(SWA_GROUP):
            o = jnp.where(lane < HEAD_DIM, og[0][r * W:(r + 1) * W], og[1][r * W:(r + 1) * W])
            o_ref[0, w * W:(w + 1) * W, r * LANES:(r + 1) * LANES] = o.astype(_BF16)


def _swa_attention(sinks, swa_q, swa_kv):
    B, S, _ = swa_q.shape
    grid = (B, S // TS_SWA)
    return pl.pallas_call(
        _swa_kernel,
        grid=grid,
        in_specs=[
            pl.BlockSpec(memory_space=pltpu.SMEM),
            pl.BlockSpec((1, TS_SWA, SWA_QW), lambda b, i: (b, i, 0)),
            pl.BlockSpec((1, S, 2 * SWA_KVW), lambda b, i: (b, 0, 0)),
        ],
        out_specs=pl.BlockSpec((1, TS_SWA, SWA_QW), lambda b, i: (b, i, 0)),
        out_shape=jax.ShapeDtypeStruct((B, S, SWA_QW), _BF16),
        compiler_params=pltpu.CompilerParams(
            dimension_semantics=("arbitrary", "arbitrary"), vmem_limit_bytes=VMEM_LIMIT),
        name="swa_attn",
    )(sinks, swa_q, swa_kv)


def _layer_norm(t, g, b):
    mu = jnp.mean(t, axis=-1, keepdims=True)
    d = t - mu
    var = jnp.mean(d * d, axis=-1, keepdims=True)
    return d * lax.rsqrt(var + LN_EPS) * g + b


def _out_kernel(x_ref, of_ref, os_ref, gate_ref, wof_ref, wos_ref, wout_ref, g1_ref, b1_ref,
                wg_ref, wu_ref, wd_ref, g2_ref, b2_ref, o_ref):
    gf = gate_ref[:, 0:D_MODEL].astype(_F32)
    gs = gate_ref[:, D_MODEL:2 * D_MODEL].astype(_F32)
    merged = gf * _dot(of_ref[...], wof_ref[...]) + gs * _dot(os_ref[...], wos_ref[...])
    mix = _dot(merged.astype(_BF16), wout_ref[...])
    h1 = _layer_norm(DEEPNORM_ALPHA * x_ref[...] + mix, g1_ref[...], b1_ref[...])
    h1b = h1.astype(_BF16)
    ff = None
    for c in range(D_FF // FF_CHUNK):
        cols = slice(c * FF_CHUNK, (c + 1) * FF_CHUNK)
        g = _dot(h1b, wg_ref[:, cols])
        u = _dot(h1b, wu_ref[:, cols])
        hc = (g * (1.0 / (1.0 + jnp.exp(-g))) * u).astype(_BF16)
        part = _dot(hc, wd_ref[cols, :])
        ff = part if ff is None else ff + part
    o_ref[...] = _layer_norm(DEEPNORM_ALPHA * h1 + ff, g2_ref[...], b2_ref[...])


def _out_ffn(x2, o_f, o_s, gates, wof, wos, wout, g1, b1, wg, wu, wd, g2, b2):
    T, D = x2.shape
    grid = (T // TM_OUT,)
    row_blk = lambda w: pl.BlockSpec((TM_OUT, w), lambda t: (t, 0))
    return pl.pallas_call(
        _out_kernel,
        grid=grid,
        in_specs=[row_blk(D), row_blk(FOX_W), row_blk(SWA_QW), row_blk(2 * D_MODEL),
                  _resident(wof.shape), _resident(wos.shape), _resident(wout.shape),
                  _resident(g1.shape), _resident(b1.shape),
                  _resident(wg.shape), _resident(wu.shape), _resident(wd.shape),
                  _resident(g2.shape), _resident(b2.shape)],
        out_specs=row_blk(D),
        out_shape=jax.ShapeDtypeStruct((T, D), _F32),
        compiler_params=pltpu.CompilerParams(
            dimension_semantics=("arbitrary",), vmem_limit_bytes=VMEM_LIMIT),
        name="out_ffn",
    )(x2, o_f, o_s, gates, wof, wos, wout, g1, b1, wg, wu, wd, g2, b2)


def _pair_swa_heads(t, axis):
    shp = t.shape
    t = t.reshape(shp[:axis] + (SWA_KV_HEADS, SWA_GROUP, HEAD_DIM) + shp[axis + 1:])
    return jnp.swapaxes(t, axis, axis + 1).reshape(shp)


def _decay_selectors():
    sel = np.zeros((FOX_HEADS // 2, LANES, LANES), np.float32)
    for hp in range(FOX_HEADS // 2):
        for hh in range(2):
            h = 2 * hp + hh
            for p, base in enumerate((_L_HI, _L_MID, _L_LO)):
                sel[hp, _L_ONE, base + h] = 1.0
                sel[hp, base + h, _L_ONE + 3 * hh + p] = -1.0
    return jnp.asarray(sel, _BF16)


def _rope_tables(S):
    half = HEAD_DIM // 2
    inv_freq = ROPE_THETA ** (-jnp.arange(0, half, dtype=_F32) / half)
    ang = jnp.arange(S, dtype=_F32)[:, None] * inv_freq[None, :]
    cos = jnp.tile(jnp.cos(ang), (1, LANES // half))
    sin = jnp.sin(ang)
    sin = jnp.tile(jnp.concatenate([-sin, sin], axis=1), (1, LANES // HEAD_DIM))
    return cos, sin


def kernel(x, w_in, b_forget, attn_sinks, w_o_fox, w_o_swa, w_out, ln1_g, ln1_b,
           w_gate, w_up, w_down, ln2_g, ln2_b):
    B, S, D = x.shape
    assert DEPTH == 1 and w_in.shape[0] == 1
    w = w_in[0]
    sizes = [FOX_W, FOX_W, FOX_W, FOX_HEADS, SWA_QW, SWA_KVW, SWA_KVW, D_MODEL, D_MODEL]
    offs = np.concatenate([[0], np.cumsum(sizes)])
    wq_f, wk_f, wv_f, w_fg, wq_s, wk_s, wv_s, wg_f, wg_s = [w[:, offs[n]:offs[n + 1]] for n in range(9)]

    w_fg3 = jnp.concatenate([w_fg] * 3 + [jnp.zeros((D, LANES - 3 * FOX_HEADS), w.dtype)], axis=1)
    w_all = jnp.concatenate(
        [wq_f * QK_SCALE, wk_f, wv_f, _pair_swa_heads(wq_s, 1) * QK_SCALE, wk_s, wv_s, wg_f, wg_s, w_fg3],
        axis=1).astype(_BF16)
    bf = jnp.concatenate([b_forget[0]] * 3 + [jnp.zeros((LANES - 3 * FOX_HEADS,), _F32)])[None, :]
    cos, sin = _rope_tables(S)

    fox_qkv, swa_q, swa_kv, gates, cs = _inproj(x, w_all, cos, sin, bf)
    o_f = _fox_attention(fox_qkv, cs, _decay_selectors())
    o_s = _swa_attention(attn_sinks[0], swa_q, swa_kv)

    T = B * S
    row = lambda v: v[0][None, :]
    out = _out_ffn(
        x.reshape(T, D), o_f.reshape(T, FOX_W), o_s.reshape(T, SWA_QW), gates.reshape(T, 2 * D_MODEL),
        w_o_fox[0].astype(_BF16), _pair_swa_heads(w_o_swa[0], 0).astype(_BF16), w_out[0].astype(_BF16),
        row(ln1_g), row(ln1_b),
        w_gate[0].astype(_BF16), w_up[0].astype(_BF16), w_down[0].astype(_BF16),
        row(ln2_g), row(ln2_b))
    return out.reshape(B, S, D)
```

```python
import functools
import math

import numpy as np
import jax
import jax.numpy as jnp
from jax import lax
from jax.experimental import pallas as pl
from jax.experimental.pallas import tpu as pltpu

D_MODEL = 1024
HEAD_DIM = 64
FOX_HEADS = 8
SWA_Q_HEADS = 8
SWA_KV_HEADS = 2
SWA_GROUP = SWA_Q_HEADS // SWA_KV_HEADS
WINDOW = 128
ROPE_THETA = 10000.0
DEPTH = 1
D_FF = -(-8 * D_MODEL // (3 * 256)) * 256
DEEPNORM_ALPHA = (2 * DEPTH) ** 0.25
LN_EPS = 1e-5

FOX_W = FOX_HEADS * HEAD_DIM
SWA_QW = SWA_Q_HEADS * HEAD_DIM
SWA_KVW = SWA_KV_HEADS * HEAD_DIM

LANES = 128
QK_SCALE = 1.0 / math.sqrt(HEAD_DIM)
LOG2E = math.log2(math.e)

_C_FOX = 0
_C_SWAQ = _C_FOX + 3 * FOX_W
_C_SWAKV = _C_SWAQ + SWA_QW
_C_GATE = _C_SWAKV + 2 * SWA_KVW
_C_FG = _C_GATE + 2 * D_MODEL
_C_END = _C_FG + LANES

_L_HI, _L_MID, _L_LO, _L_ONE = 0, FOX_HEADS, 2 * FOX_HEADS, 3 * FOX_HEADS

TM_PROJ = 512
TQ_FOX = 256
TS_SWA = 512
TM_OUT = 512
FF_CHUNK = D_FF // 2
VMEM_LIMIT = 56 * 1024 * 1024

_BF16 = jnp.bfloat16
_F32 = jnp.float32


def _dot(a, b):
    return jnp.dot(a, b, preferred_element_type=_F32)


def _dot_nt(a, b):
    return lax.dot_general(a, b, (((1,), (1,)), ((), ())), preferred_element_type=_F32)


def _resident(shape):
    nd = len(shape)
    return pl.BlockSpec(shape, lambda *_: (0,) * nd, pipeline_mode=pl.Buffered(1))


def _inproj_kernel(x_ref, w_ref, cos_ref, sin_ref, bf_ref,
                   fox_ref, swaq_ref, swakv_ref, gate_ref, cs_ref, carry_ref):
    si = pl.program_id(1)
    xb = x_ref[0].astype(_BF16)
    tm = xb.shape[0]

    fox_ref[0] = _dot(xb, w_ref[:, _C_FOX:_C_SWAQ]).astype(_BF16)

    cos = cos_ref[...]
    sin = sin_ref[...]
    lane = lax.broadcasted_iota(jnp.int32, (tm, LANES), 1)
    first_half = (lane & (HEAD_DIM - 1)) < (HEAD_DIM // 2)

    def rope(t):
        partner = jnp.where(first_half,
                            pltpu.roll(t, LANES - HEAD_DIM // 2, 1),
                            pltpu.roll(t, HEAD_DIM // 2, 1))
        return t * cos + partner * sin

    sq = _dot(xb, w_ref[:, _C_SWAQ:_C_SWAKV])
    for c in range(SWA_QW // LANES):
        swaq_ref[0, :, c * LANES:(c + 1) * LANES] = rope(sq[:, c * LANES:(c + 1) * LANES]).astype(_BF16)
    skv = _dot(xb, w_ref[:, _C_SWAKV:_C_GATE])
    swakv_ref[0, :, 0:LANES] = rope(skv[:, 0:LANES]).astype(_BF16)
    swakv_ref[0, :, LANES:2 * LANES] = skv[:, LANES:2 * LANES].astype(_BF16)

    for c in range(4):
        w = 2 * D_MODEL // 4
        g = _dot(xb, w_ref[:, _C_GATE + c * w:_C_GATE + (c + 1) * w])
        gate_ref[0, :, c * w:(c + 1) * w] = (1.0 / (1.0 + jnp.exp(-g))).astype(_BF16)

    f = _dot(xb, w_ref[:, _C_FG:_C_END]) + bf_ref[...]
    logf = (jnp.minimum(f, 0.0) - jnp.log1p(jnp.exp(-jnp.abs(f)))) * LOG2E
    row = lax.broadcasted_iota(jnp.int32, (tm, LANES), 0)
    c = logf
    k = 1
    while k < tm:
        c = c + jnp.where(row >= k, pltpu.roll(c, k, 0), 0.0)
        k *= 2

    @pl.when(si == 0)
    def _():
        carry_ref[...] = jnp.zeros_like(carry_ref)

    c = c + carry_ref[...]
    carry_ref[...] = c[tm - 1:tm, :]

    hi = c.astype(_BF16)
    r1 = c - hi.astype(_F32)
    mid = r1.astype(_BF16)
    lo = (r1 - mid.astype(_F32)).astype(_BF16)
    one = jnp.where(lane < _L_ONE + 6, 1.0, 0.0).astype(_BF16)
    cs_ref[0] = jnp.where(lane < _L_MID, hi,
                          jnp.where(lane < _L_LO, mid,
                                    jnp.where(lane < _L_ONE, lo, one)))


def _inproj(x, w_all, cos, sin, bf):
    B, S, D = x.shape
    grid = (B, S // TM_PROJ)
    row_blk = lambda w: pl.BlockSpec((1, TM_PROJ, w), lambda b, s: (b, s, 0))
    out_shapes = (
        jax.ShapeDtypeStruct((B, S, 3 * FOX_W), _BF16),
        jax.ShapeDtypeStruct((B, S, SWA_QW), _BF16),
        jax.ShapeDtypeStruct((B, S, 2 * SWA_KVW), _BF16),
        jax.ShapeDtypeStruct((B, S, 2 * D_MODEL), _BF16),
        jax.ShapeDtypeStruct((B, S, LANES), _BF16),
    )
    return pl.pallas_call(
        _inproj_kernel,
        grid=grid,
        in_specs=[
            row_blk(D),
            _resident(w_all.shape),
            pl.BlockSpec((TM_PROJ, LANES), lambda b, s: (s, 0)),
            pl.BlockSpec((TM_PROJ, LANES), lambda b, s: (s, 0)),
            _resident(bf.shape),
        ],
        out_specs=(row_blk(3 * FOX_W), row_blk(SWA_QW), row_blk(2 * SWA_KVW),
                   row_blk(2 * D_MODEL), row_blk(LANES)),
        out_shape=out_shapes,
        scratch_shapes=[pltpu.VMEM((1, LANES), _F32)],
        compiler_params=pltpu.CompilerParams(
            dimension_semantics=("arbitrary", "arbitrary"), vmem_limit_bytes=VMEM_LIMIT),
        name="inproj",
    )(x, w_all, cos, sin, bf)


def _fox_kernel(q_ref, k_ref, v_ref, cs_ref, sel_ref, o_ref):
    hp = pl.program_id(1)
    S = q_ref.shape[1]
    tq = TQ_FOX
    h0 = 2 * hp

    cs = cs_ref[0]
    ka = _dot(cs, sel_ref[hp]).astype(_BF16)
    kb = jnp.concatenate([k_ref[0], ka], axis=1)
    v = jnp.concatenate([v_ref[0], jnp.ones((S, LANES), _BF16)], axis=1)

    lane = lax.broadcasted_iota(jnp.int32, (tq, LANES), 1)
    piece = lane & (FOX_HEADS - 1)
    is_piece = lane < _L_ONE
    ones_of = jnp.where(lane < _L_ONE + 3, 0, jnp.where(lane < _L_ONE + 6, 1, -1))
    keep0 = jnp.where(is_piece, piece - h0, ones_of) == 0
    keep1 = jnp.where(is_piece, piece - h0, ones_of) == 1
    causal = ((lax.broadcasted_iota(jnp.int32, (2 * tq, tq), 0) & (tq - 1))
              >= lax.broadcasted_iota(jnp.int32, (2 * tq, tq), 1))

    for i in reversed(range(S // tq)):
        rows = slice(i * tq, (i + 1) * tq)
        w = (i + 1) * tq
        q = q_ref[0, rows, :]
        csq = cs[rows, :]
        zq = jnp.zeros_like(q)
        qa = jnp.concatenate([
            jnp.concatenate([jnp.where(lane < HEAD_DIM, q, zq), jnp.where(keep0, csq, zq)], axis=1),
            jnp.concatenate([jnp.where(lane >= HEAD_DIM, q, zq), jnp.where(keep1, csq, zq)], axis=1),
        ], axis=0)
        s = _dot_nt(qa, kb[0:w, :])
        sd = jnp.where(causal, s[:, w - tq:w], -jnp.inf)
        m = jnp.max(sd, axis=1, keepdims=True)
        if i > 0:
            so = s[:, 0:w - tq]
            m = jnp.maximum(m, jnp.max(so, axis=1, keepdims=True))
            p = jnp.concatenate([jnp.exp2(so - m), jnp.exp2(sd - m)], axis=1)
        else:
            p = jnp.exp2(sd - m)
        ol = _dot(p.astype(_BF16), v[0:w, :])
        o = ol[:, 0:LANES] / ol[:, LANES:2 * LANES]
        o_ref[0, rows, :] = jnp.where(lane < HEAD_DIM, o[0:tq], o[tq:2 * tq]).astype(_BF16)


def _fox_attention(fox_qkv, cs, sel):
    B, S, _ = fox_qkv.shape
    npair = FOX_HEADS // 2
    grid = (B, npair)
    return pl.pallas_call(
        _fox_kernel,
        grid=grid,
        in_specs=[
            pl.BlockSpec((1, S, LANES), lambda b, h: (b, 0, h)),
            pl.BlockSpec((1, S, LANES), lambda b, h: (b, 0, npair + h)),
            pl.BlockSpec((1, S, LANES), lambda b, h: (b, 0, 2 * npair + h)),
            pl.BlockSpec((1, S, LANES), lambda b, h: (b, 0, 0)),
            _resident(sel.shape),
        ],
        out_specs=pl.BlockSpec((1, S, LANES), lambda b, h: (b, 0, h)),
        out_shape=jax.ShapeDtypeStruct((B, S, FOX_W), _BF16),
        compiler_params=pltpu.CompilerParams(
            dimension_semantics=("arbitrary", "arbitrary"), vmem_limit_bytes=VMEM_LIMIT),
        name="fox_attn",
    )(fox_qkv, fox_qkv, fox_qkv, cs, sel)


def _swa_kernel(sink_ref, q_ref, kv_ref, o_ref):
    i = pl.program_id(1)
    ts = q_ref.shape[1]
    W = WINDOW
    lane = lax.broadcasted_iota(jnp.int32, (W, LANES), 1)
    nrow = SWA_GROUP * W
    qrow = lax.broadcasted_iota(jnp.int32, (nrow, 2 * W), 0) & (W - 1)
    kcol = lax.broadcasted_iota(jnp.int32, (nrow, 2 * W), 1)
    back = W + qrow - kcol
    band = jnp.where(back >= 0, jnp.where(back < WINDOW, 0.0, -jnp.inf), -jnp.inf)
    band_first = jnp.where(kcol >= W, band, -jnp.inf)

    for w in range(ts // W):
        qstart = pl.multiple_of(i * ts + w * W, W)
        pstart = pl.multiple_of(jnp.maximum(qstart - W, 0), W)
        kvw = jnp.concatenate([kv_ref[0, pl.ds(pstart, W), :], kv_ref[0, pl.ds(qstart, W), :]], axis=0)
        kk = kvw[:, 0:LANES]
        vv = jnp.concatenate([kvw[:, LANES:2 * LANES], jnp.ones((2 * W, LANES), _BF16)], axis=1)
        bias = jnp.where(i == 0, band_first, band) if w == 0 else band

        og = []
        for g in range(SWA_KV_HEADS):
            in_head = (lane < HEAD_DIM) if g == 0 else (lane >= HEAD_DIM)
            rows = []
            for r in range(SWA_GROUP):
                qb = q_ref[0, w * W:(w + 1) * W, r * LANES:(r + 1) * LANES]
                rows.append(jnp.where(in_head, qb, jnp.zeros_like(qb)))
            ql = jnp.concatenate(rows, axis=0)
            s = _dot_nt(ql, kk) + bias
            sink = jnp.concatenate([jnp.full((W, 2 * W), sink_ref[g * SWA_GROUP + r], _F32)
                                    for r in range(SWA_GROUP)], axis=0)
            m = jnp.maximum(jnp.max(s, axis=1, keepdims=True), sink)
            e = jnp.exp(s - m)
            oe = _dot(e.astype(_BF16), vv)
            den = oe[:, LANES:2 * LANES] + jnp.exp(sink[:, 0:LANES] - m[:, 0:LANES])
            og.append(oe[:, 0:LANES] / den)

        for r in range(SWA_GROUP):
            o = jnp.where(lane < HEAD_DIM, og[0][r * W:(r + 1) * W], og[1][r * W:(r + 1) * W])
            o_ref[0, w * W:(w + 1) * W, r * LANES:(r + 1) * LANES] = o.astype(_BF16)


def _swa_attention(sinks, swa_q, swa_kv):
    B, S, _ = swa_q.shape
    grid = (B, S // TS_SWA)
    return pl.pallas_call(
        _swa_kernel,
        grid=grid,
        in_specs=[
            pl.BlockSpec(memory_space=pltpu.SMEM),
            pl.BlockSpec((1, TS_SWA, SWA_QW), lambda b, i: (b, i, 0)),
            pl.BlockSpec((1, S, 2 * SWA_KVW), lambda b, i: (b, 0, 0)),
        ],
        out_specs=pl.BlockSpec((1, TS_SWA, SWA_QW), lambda b, i: (b, i, 0)),
        out_shape=jax.ShapeDtypeStruct((B, S, SWA_QW), _BF16),
        compiler_params=pltpu.CompilerParams(
            dimension_semantics=("arbitrary", "arbitrary"), vmem_limit_bytes=VMEM_LIMIT),
        name="swa_attn",
    )(sinks, swa_q, swa_kv)


def _layer_norm(t, g, b):
    mu = jnp.mean(t, axis=-1, keepdims=True)
    d = t - mu
    var = jnp.mean(d * d, axis=-1, keepdims=True)
    return d * lax.rsqrt(var + LN_EPS) * g + b


def _out_kernel(x_ref, of_ref, os_ref, gate_ref, wof_ref, wos_ref, wout_ref, g1_ref, b1_ref,
                wg_ref, wu_ref, wd_ref, g2_ref, b2_ref, o_ref):
    gf = gate_ref[:, 0:D_MODEL].astype(_F32)
    gs = gate_ref[:, D_MODEL:2 * D_MODEL].astype(_F32)
    merged = gf * _dot(of_ref[...], wof_ref[...]) + gs * _dot(os_ref[...], wos_ref[...])
    mix = _dot(merged.astype(_BF16), wout_ref[...])
    h1 = _layer_norm(DEEPNORM_ALPHA * x_ref[...] + mix, g1_ref[...], b1_ref[...])
    h1b = h1.astype(_BF16)
    ff = None
    for c in range(D_FF // FF_CHUNK):
        cols = slice(c * FF_CHUNK, (c + 1) * FF_CHUNK)
        g = _dot(h1b, wg_ref[:, cols])
        u = _dot(h1b, wu_ref[:, cols])
        hc = (g * (1.0 / (1.0 + jnp.exp(-g))) * u).astype(_BF16)
        part = _dot(hc, wd_ref[cols, :])
        ff = part if ff is None else ff + part
    o_ref[...] = _layer_norm(DEEPNORM_ALPHA * h1 + ff, g2_ref[...], b2_ref[...])


def _out_ffn(x2, o_f, o_s, gates, wof, wos, wout, g1, b1, wg, wu, wd, g2, b2):
    T, D = x2.shape
    grid = (T // TM_OUT,)
    row_blk = lambda w: pl.BlockSpec((TM_OUT, w), lambda t: (t, 0))
    return pl.pallas_call(
        _out_kernel,
        grid=grid,
        in_specs=[row_blk(D), row_blk(FOX_W), row_blk(SWA_QW), row_blk(2 * D_MODEL),
                  _resident(wof.shape), _resident(wos.shape), _resident(wout.shape),
                  _resident(g1.shape), _resident(b1.shape),
                  _resident(wg.shape), _resident(wu.shape), _resident(wd.shape),
                  _resident(g2.shape), _resident(b2.shape)],
        out_specs=row_blk(D),
        out_shape=jax.ShapeDtypeStruct((T, D), _F32),
        compiler_params=pltpu.CompilerParams(
            dimension_semantics=("arbitrary",), vmem_limit_bytes=VMEM_LIMIT),
        name="out_ffn",
    )(x2, o_f, o_s, gates, wof, wos, wout, g1, b1, wg, wu, wd, g2, b2)


def _pair_swa_heads(t, axis):
    shp = t.shape
    t = t.reshape(shp[:axis] + (SWA_KV_HEADS, SWA_GROUP, HEAD_DIM) + shp[axis + 1:])
    return jnp.swapaxes(t, axis, axis + 1).reshape(shp)


def _decay_selectors():
    sel = np.zeros((FOX_HEADS // 2, LANES, LANES), np.float32)
    for hp in range(FOX_HEADS // 2):
        for hh in range(2):
            h = 2 * hp + hh
            for p, base in enumerate((_L_HI, _L_MID, _L_LO)):
                sel[hp, _L_ONE, base + h] = 1.0
                sel[hp, base + h, _L_ONE + 3 * hh + p] = -1.0
    return jnp.asarray(sel, _BF16)


def _rope_tables(S):
    half = HEAD_DIM // 2
    inv_freq = ROPE_THETA ** (-jnp.arange(0, half, dtype=_F32) / half)
    ang = jnp.arange(S, dtype=_F32)[:, None] * inv_freq[None, :]
    cos = jnp.tile(jnp.cos(ang), (1, LANES // half))
    sin = jnp.sin(ang)
    sin = jnp.tile(jnp.concatenate([-sin, sin], axis=1), (1, LANES // HEAD_DIM))
    return cos, sin


def kernel(x, w_in, b_forget, attn_sinks, w_o_fox, w_o_swa, w_out, ln1_g, ln1_b,
           w_gate, w_up, w_down, ln2_g, ln2_b):
    B, S, D = x.shape
    assert DEPTH == 1 and w_in.shape[0] == 1
    w = w_in[0]
    sizes = [FOX_W, FOX_W, FOX_W, FOX_HEADS, SWA_QW, SWA_KVW, SWA_KVW, D_MODEL, D_MODEL]
    offs = np.concatenate([[0], np.cumsum(sizes)])
    wq_f, wk_f, wv_f, w_fg, wq_s, wk_s, wv_s, wg_f, wg_s = [w[:, offs[n]:offs[n + 1]] for n in range(9)]

    w_fg3 = jnp.concatenate([w_fg] * 3 + [jnp.zeros((D, LANES - 3 * FOX_HEADS), w.dtype)], axis=1)
    w_all = jnp.concatenate(
        [wq_f * (QK_SCALE * LOG2E), wk_f, wv_f, _pair_swa_heads(wq_s, 1) * QK_SCALE, wk_s, wv_s, wg_f, wg_s, w_fg3],
        axis=1).astype(_BF16)
    bf = jnp.concatenate([b_forget[0]] * 3 + [jnp.zeros((LANES - 3 * FOX_HEADS,), _F32)])[None, :]
    cos, sin = _rope_tables(S)

    fox_qkv, swa_q, swa_kv, gates, cs = _inproj(x, w_all, cos, sin, bf)
    o_f = _fox_attention(fox_qkv, cs, _decay_selectors())
    o_s = _swa_attention(attn_sinks[0], swa_q, swa_kv)

    T = B * S
    row = lambda v: v[0][None, :]
    out = _out_ffn(
        x.reshape(T, D), o_f.reshape(T, FOX_W), o_s.reshape(T, SWA_QW), gates.reshape(T, 2 * D_MODEL),
        w_o_fox[0].astype(_BF16), _pair_swa_heads(w_o_swa[0], 0).astype(_BF16), w_out[0].astype(_BF16),
        row(ln1_g), row(ln1_b),
        w_gate[0].astype(_BF16), w_up[0].astype(_BF16), w_down[0].astype(_BF16),
        row(ln2_g), row(ln2_b))
    return out.reshape(B, S, D)
```

```python
import functools
import math

import numpy as np
import jax
import jax.numpy as jnp
from jax import lax
from jax.experimental import pallas as pl
from jax.experimental.pallas import tpu as pltpu

D_MODEL = 1024
HEAD_DIM = 64
FOX_HEADS = 8
SWA_Q_HEADS = 8
SWA_KV_HEADS = 2
SWA_GROUP = SWA_Q_HEADS // SWA_KV_HEADS
WINDOW = 128
ROPE_THETA = 10000.0
DEPTH = 1
D_FF = -(-8 * D_MODEL // (3 * 256)) * 256
DEEPNORM_ALPHA = (2 * DEPTH) ** 0.25
LN_EPS = 1e-5

FOX_W = FOX_HEADS * HEAD_DIM
SWA_QW = SWA_Q_HEADS * HEAD_DIM
SWA_KVW = SWA_KV_HEADS * HEAD_DIM

LANES = 128
QK_SCALE = 1.0 / math.sqrt(HEAD_DIM)
LOG2E = math.log2(math.e)

_C_FOX = 0
_C_SWAQ = _C_FOX + 3 * FOX_W
_C_SWAKV = _C_SWAQ + SWA_QW
_C_GATE = _C_SWAKV + 2 * SWA_KVW
_C_FG = _C_GATE + 2 * D_MODEL
_C_END = _C_FG + LANES

_L_HI, _L_MID, _L_LO, _L_ONE = 0, FOX_HEADS, 2 * FOX_HEADS, 3 * FOX_HEADS

TM_PROJ = 512
TQ_FOX = 256
TS_SWA = 512
TM_OUT = 512
MXU_TILE = 256
FF_BOUNDS = (0, 6 * MXU_TILE, D_FF)
assert all(b % MXU_TILE == 0 for b in FF_BOUNDS)
VMEM_LIMIT = 56 * 1024 * 1024

_BF16 = jnp.bfloat16
_F32 = jnp.float32


def _dot(a, b):
    return jnp.dot(a, b, preferred_element_type=_F32)


def _dot_nt(a, b):
    return lax.dot_general(a, b, (((1,), (1,)), ((), ())), preferred_element_type=_F32)


def _resident(shape):
    nd = len(shape)
    return pl.BlockSpec(shape, lambda *_: (0,) * nd, pipeline_mode=pl.Buffered(1))


def _inproj_kernel(x_ref, w_ref, cos_ref, sin_ref, bf_ref,
                   fox_ref, swaq_ref, swakv_ref, gate_ref, cs_ref, carry_ref):
    @pl.when(pl.program_id(1) == 0)
    def _():
        carry_ref[...] = jnp.zeros_like(carry_ref)

    xb = x_ref[0].astype(_BF16)
    tm = xb.shape[0]
    lane = lax.broadcasted_iota(jnp.int32, (tm, LANES), 1)

    f = _dot(xb, w_ref[:, _C_FG:_C_END]) + bf_ref[...]
    logf = (jnp.minimum(f, 0.0) - jnp.log1p(jnp.exp(-jnp.abs(f)))) * LOG2E
    row = lax.broadcasted_iota(jnp.int32, (tm, LANES), 0)
    c = logf
    k = 1
    while k < tm:
        c = c + jnp.where(row >= k, pltpu.roll(c, k, 0), 0.0)
        k *= 2
    c = c + carry_ref[...]
    carry_ref[...] = c[tm - 1:tm, :]

    hi = c.astype(_BF16)
    r1 = c - hi.astype(_F32)
    mid = r1.astype(_BF16)
    lo = (r1 - mid.astype(_F32)).astype(_BF16)
    one = jnp.where(lane < _L_ONE + 6, 1.0, 0.0).astype(_BF16)
    cs_ref[0] = jnp.where(lane < _L_MID, hi,
                          jnp.where(lane < _L_LO, mid,
                                    jnp.where(lane < _L_ONE, lo, one)))

    fox_ref[0] = _dot(xb, w_ref[:, _C_FOX:_C_SWAQ]).astype(_BF16)

    cos = cos_ref[...]
    sin = sin_ref[...]
    first_half = (lane & (HEAD_DIM - 1)) < (HEAD_DIM // 2)

    def rope(t):
        partner = jnp.where(first_half,
                            pltpu.roll(t, LANES - HEAD_DIM // 2, 1),
                            pltpu.roll(t, HEAD_DIM // 2, 1))
        return t * cos + partner * sin

    sq = _dot(xb, w_ref[:, _C_SWAQ:_C_SWAKV])
    for c in range(SWA_QW // LANES):
        swaq_ref[0, :, c * LANES:(c + 1) * LANES] = rope(sq[:, c * LANES:(c + 1) * LANES]).astype(_BF16)
    skv = _dot(xb, w_ref[:, _C_SWAKV:_C_GATE])
    swakv_ref[0, :, 0:LANES] = rope(skv[:, 0:LANES]).astype(_BF16)
    swakv_ref[0, :, LANES:2 * LANES] = skv[:, LANES:2 * LANES].astype(_BF16)

    for c in range(4):
        w = 2 * D_MODEL // 4
        g = _dot(xb, w_ref[:, _C_GATE + c * w:_C_GATE + (c + 1) * w])
        gate_ref[0, :, c * w:(c + 1) * w] = (1.0 / (1.0 + jnp.exp(-g))).astype(_BF16)


def _inproj(x, w_all, cos, sin, bf):
    B, S, D = x.shape
    grid = (B, S // TM_PROJ)
    row_blk = lambda w: pl.BlockSpec((1, TM_PROJ, w), lambda b, s: (b, s, 0))
    out_shapes = (
        jax.ShapeDtypeStruct((B, S, 3 * FOX_W), _BF16),
        jax.ShapeDtypeStruct((B, S, SWA_QW), _BF16),
        jax.ShapeDtypeStruct((B, S, 2 * SWA_KVW), _BF16),
        jax.ShapeDtypeStruct((B, S, 2 * D_MODEL), _BF16),
        jax.ShapeDtypeStruct((B, S, LANES), _BF16),
    )
    return pl.pallas_call(
        _inproj_kernel,
        grid=grid,
        in_specs=[
            row_blk(D),
            _resident(w_all.shape),
            pl.BlockSpec((TM_PROJ, LANES), lambda b, s: (s, 0)),
            pl.BlockSpec((TM_PROJ, LANES), lambda b, s: (s, 0)),
            _resident(bf.shape),
        ],
        out_specs=(row_blk(3 * FOX_W), row_blk(SWA_QW), row_blk(2 * SWA_KVW),
                   row_blk(2 * D_MODEL), row_blk(LANES)),
        out_shape=out_shapes,
        scratch_shapes=[pltpu.VMEM((1, LANES), _F32)],
        compiler_params=pltpu.CompilerParams(
            dimension_semantics=("arbitrary", "arbitrary"), vmem_limit_bytes=VMEM_LIMIT),
        name="inproj",
    )(x, w_all, cos, sin, bf)


def _fox_kernel(q_ref, k_ref, v_ref, cs_ref, sel_ref, o_ref):
    hp = pl.program_id(1)
    S = q_ref.shape[1]
    tq = TQ_FOX
    h0 = 2 * hp

    cs = cs_ref[0]
    ka = _dot(cs, sel_ref[hp]).astype(_BF16)
    kb = jnp.concatenate([k_ref[0], ka], axis=1)
    v = jnp.concatenate([v_ref[0], jnp.ones((S, LANES), _BF16)], axis=1)

    lane = lax.broadcasted_iota(jnp.int32, (tq, LANES), 1)
    piece = lane & (FOX_HEADS - 1)
    is_piece = lane < _L_ONE
    ones_of = jnp.where(lane < _L_ONE + 3, 0, jnp.where(lane < _L_ONE + 6, 1, -1))
    keep0 = jnp.where(is_piece, piece - h0, ones_of) == 0
    keep1 = jnp.where(is_piece, piece - h0, ones_of) == 1
    causal = ((lax.broadcasted_iota(jnp.int32, (2 * tq, tq), 0) & (tq - 1))
              >= lax.broadcasted_iota(jnp.int32, (2 * tq, tq), 1))

    for i in reversed(range(S // tq)):
        rows = slice(i * tq, (i + 1) * tq)
        w = (i + 1) * tq
        q = q_ref[0, rows, :]
        csq = cs[rows, :]
        zq = jnp.zeros_like(q)
        qa = jnp.concatenate([
            jnp.concatenate([jnp.where(lane < HEAD_DIM, q, zq), jnp.where(keep0, csq, zq)], axis=1),
            jnp.concatenate([jnp.where(lane >= HEAD_DIM, q, zq), jnp.where(keep1, csq, zq)], axis=1),
        ], axis=0)
        s = _dot_nt(qa, kb[0:w, :])
        sd = jnp.where(causal, s[:, w - tq:w], -jnp.inf)
        m = jnp.max(sd, axis=1, keepdims=True)
        if i > 0:
            so = s[:, 0:w - tq]
            m = jnp.maximum(m, jnp.max(so, axis=1, keepdims=True))
            p = jnp.concatenate([jnp.exp2(so - m), jnp.exp2(sd - m)], axis=1)
        else:
            p = jnp.exp2(sd - m)
        ol = _dot(p.astype(_BF16), v[0:w, :])
        o = ol[:, 0:LANES] / ol[:, LANES:2 * LANES]
        o_ref[0, rows, :] = jnp.where(lane < HEAD_DIM, o[0:tq], o[tq:2 * tq]).astype(_BF16)


def _fox_attention(fox_qkv, cs, sel):
    B, S, _ = fox_qkv.shape
    npair = FOX_HEADS // 2
    grid = (B, npair)
    return pl.pallas_call(
        _fox_kernel,
        grid=grid,
        in_specs=[
            pl.BlockSpec((1, S, LANES), lambda b, h: (b, 0, h)),
            pl.BlockSpec((1, S, LANES), lambda b, h: (b, 0, npair + h)),
            pl.BlockSpec((1, S, LANES), lambda b, h: (b, 0, 2 * npair + h)),
            pl.BlockSpec((1, S, LANES), lambda b, h: (b, 0, 0)),
            _resident(sel.shape),
        ],
        out_specs=pl.BlockSpec((1, S, LANES), lambda b, h: (b, 0, h)),
        out_shape=jax.ShapeDtypeStruct((B, S, FOX_W), _BF16),
        compiler_params=pltpu.CompilerParams(
            dimension_semantics=("arbitrary", "arbitrary"), vmem_limit_bytes=VMEM_LIMIT),
        name="fox_attn",
    )(fox_qkv, fox_qkv, fox_qkv, cs, sel)


def _swa_kernel(sink_ref, q_ref, kv_ref, o_ref):
    i = pl.program_id(1)
    ts = q_ref.shape[1]
    W = WINDOW
    lane = lax.broadcasted_iota(jnp.int32, (W, LANES), 1)
    nrow = SWA_GROUP * W
    qrow = lax.broadcasted_iota(jnp.int32, (nrow, 2 * W), 0) & (W - 1)
    kcol = lax.broadcasted_iota(jnp.int32, (nrow, 2 * W), 1)
    back = W + qrow - kcol
    band = jnp.where(back >= 0, jnp.where(back < WINDOW, 0.0, -jnp.inf), -jnp.inf)
    band_first = jnp.where(kcol >= W, band, -jnp.inf)

    for w in range(ts // W):
        qstart = pl.multiple_of(i * ts + w * W, W)
        pstart = pl.multiple_of(jnp.maximum(qstart - W, 0), W)
        kvw = jnp.concatenate([kv_ref[0, pl.ds(pstart, W), :], kv_ref[0, pl.ds(qstart, W), :]], axis=0)
        kk = kvw[:, 0:LANES]
        vv = jnp.concatenate([kvw[:, LANES:2 * LANES], jnp.ones((2 * W, LANES), _BF16)], axis=1)
        bias = jnp.where(i == 0, band_first, band) if w == 0 else band

        og = []
        for g in range(SWA_KV_HEADS):
            in_head = (lane < HEAD_DIM) if g == 0 else (lane >= HEAD_DIM)
            rows = []
            for r in range(SWA_GROUP):
                qb = q_ref[0, w * W:(w + 1) * W, r * LANES:(r + 1) * LANES]
                rows.append(jnp.where(in_head, qb, jnp.zeros_like(qb)))
            ql = jnp.concatenate(rows, axis=0)
            s = _dot_nt(ql, kk) + bias
            sink = jnp.concatenate([jnp.full((W, 2 * W), sink_ref[g * SWA_GROUP + r], _F32)
                                    for r in range(SWA_GROUP)], axis=0)
            m = jnp.maximum(jnp.max(s, axis=1, keepdims=True), sink)
            e = jnp.exp(s - m)
            oe = _dot(e.astype(_BF16), vv)
            den = oe[:, LANES:2 * LANES] + jnp.exp(sink[:, 0:LANES] - m[:, 0:LANES])
            og.append(oe[:, 0:LANES] / den)

        for r in range(SWA_GROUP):
            o = jnp.where(lane < HEAD_DIM, og[0][r * W:(r + 1) * W], og[1][r * W:(r + 1) * W])
            o_ref[0, w * W:(w + 1) * W, r * LANES:(r + 1) * LANES] = o.astype(_BF16)


def _swa_attention(sinks, swa_q, swa_kv):
    B, S, _ = swa_q.shape
    grid = (B, S // TS_SWA)
    return pl.pallas_call(
        _swa_kernel,
        grid=grid,
        in_specs=[
            pl.BlockSpec(memory_space=pltpu.SMEM),
            pl.BlockSpec((1, TS_SWA, SWA_QW), lambda b, i: (b, i, 0)),
            pl.BlockSpec((1, S, 2 * SWA_KVW), lambda b, i: (b, 0, 0)),
        ],
        out_specs=pl.BlockSpec((1, TS_SWA, SWA_QW), lambda b, i: (b, i, 0)),
        out_shape=jax.ShapeDtypeStruct((B, S, SWA_QW), _BF16),
        compiler_params=pltpu.CompilerParams(
            dimension_semantics=("arbitrary", "arbitrary"), vmem_limit_bytes=VMEM_LIMIT),
        name="swa_attn",
    )(sinks, swa_q, swa_kv)


def _layer_norm(t, g, b):
    mu = jnp.mean(t, axis=-1, keepdims=True)
    d = t - mu
    var = jnp.mean(d * d, axis=-1, keepdims=True)
    return d * lax.rsqrt(var + LN_EPS) * g + b


def _out_kernel(x_ref, of_ref, os_ref, gate_ref, wof_ref, wos_ref, wout_ref, g1_ref, b1_ref,
                wg_ref, wu_ref, wd_ref, g2_ref, b2_ref, o_ref, pre1_ref, pre2_ref):
    @pl.when(pl.program_id(0) == 0)
    def _():
        pre1_ref[...] = jnp.zeros_like(pre1_ref)
        pre2_ref[...] = jnp.zeros_like(pre2_ref)

    of_p = _dot(of_ref[...], wof_ref[...])
    os_p = _dot(os_ref[...], wos_ref[...])

    h1 = _layer_norm(pre1_ref[...], g1_ref[...], b1_ref[...])
    lhs = h1.astype(_BF16)

    def gate_up(c):
        cols = slice(FF_BOUNDS[c], FF_BOUNDS[c + 1])
        return _dot(lhs, wg_ref[:, cols]), _dot(lhs, wu_ref[:, cols])

    def down(c, g, u):
        hc = (g * (1.0 / (1.0 + jnp.exp(-g))) * u).astype(_BF16)
        return _dot(hc, wd_ref[FF_BOUNDS[c]:FF_BOUNDS[c + 1], :])

    g0, u0 = gate_up(0)
    merged = (gate_ref[:, 0:D_MODEL].astype(_F32) * of_p
              + gate_ref[:, D_MODEL:2 * D_MODEL].astype(_F32) * os_p)
    mix = _dot(merged.astype(_BF16), wout_ref[...])
    ff = down(0, g0, u0)

    o_ref[...] = _layer_norm(pre2_ref[...], g2_ref[...], b2_ref[...])

    for c in range(1, len(FF_BOUNDS) - 1):
        g, u = gate_up(c)
        ff = ff + down(c, g, u)

    pre1_ref[...] = DEEPNORM_ALPHA * x_ref[...] + mix
    pre2_ref[...] = DEEPNORM_ALPHA * h1 + ff


def _out_ffn(x2, o_f, o_s, gates, wof, wos, wout, g1, b1, wg, wu, wd, g2, b2):
    T, D = x2.shape
    nblk = T // TM_OUT
    grid = (nblk + 2,)
    in_blk = lambda w: pl.BlockSpec((TM_OUT, w), lambda t: (jnp.minimum(t, nblk - 1), 0))
    return pl.pallas_call(
        _out_kernel,
        grid=grid,
        in_specs=[in_blk(D), in_blk(FOX_W), in_blk(SWA_QW), in_blk(2 * D_MODEL),
                  _resident(wof.shape), _resident(wos.shape), _resident(wout.shape),
                  _resident(g1.shape), _resident(b1.shape),
                  _resident(wg.shape), _resident(wu.shape), _resident(wd.shape),
                  _resident(g2.shape), _resident(b2.shape)],
        out_specs=pl.BlockSpec((TM_OUT, D), lambda t: (jnp.maximum(t - 2, 0), 0)),
        out_shape=jax.ShapeDtypeStruct((T, D), _F32),
        scratch_shapes=[pltpu.VMEM((TM_OUT, D), _F32), pltpu.VMEM((TM_OUT, D), _F32)],
        compiler_params=pltpu.CompilerParams(
            dimension_semantics=("arbitrary",), vmem_limit_bytes=VMEM_LIMIT),
        name="out_ffn",
    )(x2, o_f, o_s, gates, wof, wos, wout, g1, b1, wg, wu, wd, g2, b2)


def _pair_swa_heads(t, axis):
    shp = t.shape
    t = t.reshape(shp[:axis] + (SWA_KV_HEADS, SWA_GROUP, HEAD_DIM) + shp[axis + 1:])
    return jnp.swapaxes(t, axis, axis + 1).reshape(shp)


def _decay_selectors():
    sel = np.zeros((FOX_HEADS // 2, LANES, LANES), np.float32)
    for hp in range(FOX_HEADS // 2):
        for hh in range(2):
            h = 2 * hp + hh
            for p, base in enumerate((_L_HI, _L_MID, _L_LO)):
                sel[hp, _L_ONE, base + h] = 1.0
                sel[hp, base + h, _L_ONE + 3 * hh + p] = -1.0
    return jnp.asarray(sel, _BF16)


def _rope_tables(S):
    half = HEAD_DIM // 2
    inv_freq = ROPE_THETA ** (-jnp.arange(0, half, dtype=_F32) / half)
    ang = jnp.arange(S, dtype=_F32)[:, None] * inv_freq[None, :]
    cos = jnp.tile(jnp.cos(ang), (1, LANES // half))
    sin = jnp.sin(ang)
    sin = jnp.tile(jnp.concatenate([-sin, sin], axis=1), (1, LANES // HEAD_DIM))
    return cos, sin


def kernel(x, w_in, b_forget, attn_sinks, w_o_fox, w_o_swa, w_out, ln1_g, ln1_b,
           w_gate, w_up, w_down, ln2_g, ln2_b):
    B, S, D = x.shape
    assert DEPTH == 1 and w_in.shape[0] == 1
    w = w_in[0]
    sizes = [FOX_W, FOX_W, FOX_W, FOX_HEADS, SWA_QW, SWA_KVW, SWA_KVW, D_MODEL, D_MODEL]
    offs = np.concatenate([[0], np.cumsum(sizes)])
    wq_f, wk_f, wv_f, w_fg, wq_s, wk_s, wv_s, wg_f, wg_s = [w[:, offs[n]:offs[n + 1]] for n in range(9)]

    w_fg3 = jnp.concatenate([w_fg] * 3 + [jnp.zeros((D, LANES - 3 * FOX_HEADS), w.dtype)], axis=1)
    w_all = jnp.concatenate(
        [wq_f * (QK_SCALE * LOG2E), wk_f, wv_f, _pair_swa_heads(wq_s, 1) * QK_SCALE, wk_s, wv_s, wg_f, wg_s, w_fg3],
        axis=1).astype(_BF16)
    bf = jnp.concatenate([b_forget[0]] * 3 + [jnp.zeros((LANES - 3 * FOX_HEADS,), _F32)])[None, :]
    cos, sin = _rope_tables(S)

    fox_qkv, swa_q, swa_kv, gates, cs = _inproj(x, w_all, cos, sin, bf)
    o_f = _fox_attention(fox_qkv, cs, _decay_selectors())
    o_s = _swa_attention(attn_sinks[0], swa_q, swa_kv)

    T = B * S
    row = lambda v: v[0][None, :]
    out = _out_ffn(
        x.reshape(T, D), o_f.reshape(T, FOX_W), o_s.reshape(T, SWA_QW), gates.reshape(T, 2 * D_MODEL),
        w_o_fox[0].astype(_BF16), _pair_swa_heads(w_o_swa[0], 0).astype(_BF16), w_out[0].astype(_BF16),
        row(ln1_g), row(ln1_b),
        w_gate[0].astype(_BF16), w_up[0].astype(_BF16), w_down[0].astype(_BF16),
        row(ln2_g), row(ln2_b))
    return out.reshape(B, S, D)
```

```python
import functools
import math

import numpy as np
import jax
import jax.numpy as jnp
from jax import lax
from jax.experimental import pallas as pl
from jax.experimental.pallas import tpu as pltpu

D_MODEL = 1024
HEAD_DIM = 64
FOX_HEADS = 8
SWA_Q_HEADS = 8
SWA_KV_HEADS = 2
SWA_GROUP = SWA_Q_HEADS // SWA_KV_HEADS
WINDOW = 128
ROPE_THETA = 10000.0
DEPTH = 1
D_FF = -(-8 * D_MODEL // (3 * 256)) * 256
DEEPNORM_ALPHA = (2 * DEPTH) ** 0.25
LN_EPS = 1e-5

FOX_W = FOX_HEADS * HEAD_DIM
SWA_QW = SWA_Q_HEADS * HEAD_DIM
SWA_KVW = SWA_KV_HEADS * HEAD_DIM

LANES = 128
QK_SCALE = 1.0 / math.sqrt(HEAD_DIM)
LOG2E = math.log2(math.e)

_C_FOX = 0
_C_SWAQ = _C_FOX + 3 * FOX_W
_C_SWAKV = _C_SWAQ + SWA_QW
_C_GATE = _C_SWAKV + 2 * SWA_KVW
_C_FG = _C_GATE + 2 * D_MODEL
_C_END = _C_FG + LANES

PAIR_LANES = LANES // (FOX_HEADS // 2)
SIDE_LANES = 12


def _side_lane_fields(lane):
    j = lane & (PAIR_LANES - 1)
    triple = (j * 11) >> 5
    return lane // PAIR_LANES, j, triple, j - 3 * triple

TM_PROJ = 1024
TQ_FOX = 256
TS_SWA = 512
TM_OUT = 512
MXU_TILE = 256
FF_BOUNDS = (0, 6 * MXU_TILE, D_FF)
assert all(b % MXU_TILE == 0 for b in FF_BOUNDS)
VMEM_LIMIT = 56 * 1024 * 1024

_BF16 = jnp.bfloat16
_F32 = jnp.float32


def _dot(a, b):
    return jnp.dot(a, b, preferred_element_type=_F32)


def _dot_nt(a, b):
    return lax.dot_general(a, b, (((1,), (1,)), ((), ())), preferred_element_type=_F32)


def _resident(shape):
    nd = len(shape)
    return pl.BlockSpec(shape, lambda *_: (0,) * nd, pipeline_mode=pl.Buffered(1))


def _inproj_kernel(x_ref, w_ref, cos_ref, sin_ref, bf_ref,
                   fox_ref, swaq_ref, swakv_ref, gate_ref, csq_ref, csk_ref, carry_ref):
    @pl.when(pl.program_id(1) == 0)
    def _():
        carry_ref[...] = jnp.zeros_like(carry_ref)

    xb = x_ref[0].astype(_BF16)
    tm = xb.shape[0]
    lane = lax.broadcasted_iota(jnp.int32, (tm, LANES), 1)

    f = _dot(xb, w_ref[:, _C_FG:_C_END]) + bf_ref[...]
    fox_ref[0] = _dot(xb, w_ref[:, _C_FOX:_C_SWAQ]).astype(_BF16)

    logf = (jnp.minimum(f, 0.0) - jnp.log1p(jnp.exp(-jnp.abs(f)))) * LOG2E
    row = lax.broadcasted_iota(jnp.int32, (tm, LANES), 0)
    c = logf
    k = 1
    while k < tm:
        c = c + jnp.where(row >= k, pltpu.roll(c, k, 0), 0.0)
        k *= 2
    c = c + carry_ref[...]
    carry_ref[...] = c[tm - 1:tm, :]

    hi = c.astype(_BF16)
    r1 = c - hi.astype(_F32)
    mid = r1.astype(_BF16)
    lo = (r1 - mid.astype(_F32)).astype(_BF16)
    _, j, triple, part = _side_lane_fields(lane)
    pieces = jnp.where(part == 0, hi, jnp.where(part == 1, mid, lo))
    one = jnp.ones_like(hi)
    zero = jnp.zeros_like(hi)
    used = j < SIDE_LANES
    csq_ref[0] = jnp.where(used, jnp.where(triple < 2, pieces, one), zero)
    csk_ref[0] = jnp.where(used, jnp.where(triple < 2, one, -pieces), zero)

    cos = cos_ref[...]
    sin = sin_ref[...]
    first_half = (lane & (HEAD_DIM - 1)) < (HEAD_DIM // 2)

    def rope(t):
        partner = jnp.where(first_half,
                            pltpu.roll(t, LANES - HEAD_DIM // 2, 1),
                            pltpu.roll(t, HEAD_DIM // 2, 1))
        return t * cos + partner * sin

    sq = _dot(xb, w_ref[:, _C_SWAQ:_C_SWAKV])
    for c in range(SWA_QW // LANES):
        swaq_ref[0, :, c * LANES:(c + 1) * LANES] = rope(sq[:, c * LANES:(c + 1) * LANES]).astype(_BF16)
    skv = _dot(xb, w_ref[:, _C_SWAKV:_C_GATE])
    swakv_ref[0, :, 0:LANES] = rope(skv[:, 0:LANES]).astype(_BF16)
    swakv_ref[0, :, LANES:2 * LANES] = skv[:, LANES:2 * LANES].astype(_BF16)

    for c in range(4):
        w = 2 * D_MODEL // 4
        g = _dot(xb, w_ref[:, _C_GATE + c * w:_C_GATE + (c + 1) * w])
        gate_ref[0, :, c * w:(c + 1) * w] = (1.0 / (1.0 + jnp.exp(-g))).astype(_BF16)


def _inproj(x, w_all, cos, sin, bf):
    B, S, D = x.shape
    grid = (B, S // TM_PROJ)
    row_blk = lambda w: pl.BlockSpec((1, TM_PROJ, w), lambda b, s: (b, s, 0))
    out_shapes = (
        jax.ShapeDtypeStruct((B, S, 3 * FOX_W), _BF16),
        jax.ShapeDtypeStruct((B, S, SWA_QW), _BF16),
        jax.ShapeDtypeStruct((B, S, 2 * SWA_KVW), _BF16),
        jax.ShapeDtypeStruct((B, S, 2 * D_MODEL), _BF16),
        jax.ShapeDtypeStruct((B, S, LANES), _BF16),
        jax.ShapeDtypeStruct((B, S, LANES), _BF16),
    )
    return pl.pallas_call(
        _inproj_kernel,
        grid=grid,
        in_specs=[
            row_blk(D),
            _resident(w_all.shape),
            pl.BlockSpec((TM_PROJ, LANES), lambda b, s: (s, 0)),
            pl.BlockSpec((TM_PROJ, LANES), lambda b, s: (s, 0)),
            _resident(bf.shape),
        ],
        out_specs=(row_blk(3 * FOX_W), row_blk(SWA_QW), row_blk(2 * SWA_KVW),
                   row_blk(2 * D_MODEL), row_blk(LANES), row_blk(LANES)),
        out_shape=out_shapes,
        scratch_shapes=[pltpu.VMEM((1, LANES), _F32)],
        compiler_params=pltpu.CompilerParams(
            dimension_semantics=("arbitrary", "arbitrary"), vmem_limit_bytes=VMEM_LIMIT),
        name="inproj",
    )(x, w_all, cos, sin, bf)


def _fox_kernel(q_ref, k_ref, v_ref, csq_ref, csk_ref, o_ref):
    hp = pl.program_id(1)
    S = q_ref.shape[1]
    tq = TQ_FOX

    pair_k = lax.broadcasted_iota(jnp.int32, (S, LANES), 1) // PAIR_LANES
    csk = csk_ref[0]
    kb = jnp.concatenate([k_ref[0], jnp.where(pair_k == hp, csk, jnp.zeros_like(csk))], axis=1)
    v = jnp.concatenate([v_ref[0], jnp.ones((S, LANES), _BF16)], axis=1)

    lane = lax.broadcasted_iota(jnp.int32, (tq, LANES), 1)
    pair, j, triple, _ = _side_lane_fields(lane)
    head_of = jnp.where(pair == hp, jnp.where(j < SIDE_LANES, triple & 1, -1), -1)
    keep0 = head_of == 0
    keep1 = head_of == 1
    causal = ((lax.broadcasted_iota(jnp.int32, (2 * tq, tq), 0) & (tq - 1))
              >= lax.broadcasted_iota(jnp.int32, (2 * tq, tq), 1))

    for i in reversed(range(S // tq)):
        rows = slice(i * tq, (i + 1) * tq)
        w = (i + 1) * tq
        q = q_ref[0, rows, :]
        csq = csq_ref[0, rows, :]
        zq = jnp.zeros_like(q)
        qa = jnp.concatenate([
            jnp.concatenate([jnp.where(lane < HEAD_DIM, q, zq), jnp.where(keep0, csq, zq)], axis=1),
            jnp.concatenate([jnp.where(lane >= HEAD_DIM, q, zq), jnp.where(keep1, csq, zq)], axis=1),
        ], axis=0)
        s = _dot_nt(qa, kb[0:w, :])
        sd = jnp.where(causal, s[:, w - tq:w], -jnp.inf)
        m = jnp.max(sd, axis=1, keepdims=True)
        if i > 0:
            so = s[:, 0:w - tq]
            m = jnp.maximum(m, jnp.max(so, axis=1, keepdims=True))
            p = jnp.concatenate([jnp.exp2(so - m), jnp.exp2(sd - m)], axis=1)
        else:
            p = jnp.exp2(sd - m)
        ol = _dot(p.astype(_BF16), v[0:w, :])
        o = ol[:, 0:LANES] / ol[:, LANES:2 * LANES]
        o_ref[0, rows, :] = jnp.where(lane < HEAD_DIM, o[0:tq], o[tq:2 * tq]).astype(_BF16)


def _fox_attention(fox_qkv, csq, csk):
    B, S, _ = fox_qkv.shape
    npair = FOX_HEADS // 2
    grid = (B, npair)
    return pl.pallas_call(
        _fox_kernel,
        grid=grid,
        in_specs=[
            pl.BlockSpec((1, S, LANES), lambda b, h: (b, 0, h)),
            pl.BlockSpec((1, S, LANES), lambda b, h: (b, 0, npair + h)),
            pl.BlockSpec((1, S, LANES), lambda b, h: (b, 0, 2 * npair + h)),
            pl.BlockSpec((1, S, LANES), lambda b, h: (b, 0, 0)),
            pl.BlockSpec((1, S, LANES), lambda b, h: (b, 0, 0)),
        ],
        out_specs=pl.BlockSpec((1, S, LANES), lambda b, h: (b, 0, h)),
        out_shape=jax.ShapeDtypeStruct((B, S, FOX_W), _BF16),
        compiler_params=pltpu.CompilerParams(
            dimension_semantics=("arbitrary", "arbitrary"), vmem_limit_bytes=VMEM_LIMIT),
        name="fox_attn",
    )(fox_qkv, fox_qkv, fox_qkv, csq, csk)


def _swa_kernel(sink_ref, q_ref, kv_ref, o_ref):
    i = pl.program_id(1)
    ts = q_ref.shape[1]
    W = WINDOW
    lane = lax.broadcasted_iota(jnp.int32, (W, LANES), 1)
    nrow = SWA_GROUP * W
    qrow = lax.broadcasted_iota(jnp.int32, (nrow, 2 * W), 0) & (W - 1)
    kcol = lax.broadcasted_iota(jnp.int32, (nrow, 2 * W), 1)
    back = W + qrow - kcol
    band = jnp.where(back >= 0, jnp.where(back < WINDOW, 0.0, -jnp.inf), -jnp.inf)
    band_first = jnp.where(kcol >= W, band, -jnp.inf)

    for w in range(ts // W):
        qstart = pl.multiple_of(i * ts + w * W, W)
        pstart = pl.multiple_of(jnp.maximum(qstart - W, 0), W)
        kvw = jnp.concatenate([kv_ref[0, pl.ds(pstart, W), :], kv_ref[0, pl.ds(qstart, W), :]], axis=0)
        kk = kvw[:, 0:LANES]
        vv = jnp.concatenate([kvw[:, LANES:2 * LANES], jnp.ones((2 * W, LANES), _BF16)], axis=1)
        bias = jnp.where(i == 0, band_first, band) if w == 0 else band

        og = []
        for g in range(SWA_KV_HEADS):
            in_head = (lane < HEAD_DIM) if g == 0 else (lane >= HEAD_DIM)
            rows = []
            for r in range(SWA_GROUP):
                qb = q_ref[0, w * W:(w + 1) * W, r * LANES:(r + 1) * LANES]
                rows.append(jnp.where(in_head, qb, jnp.zeros_like(qb)))
            ql = jnp.concatenate(rows, axis=0)
            s = _dot_nt(ql, kk) + bias
            sink = jnp.concatenate([jnp.full((W, 2 * W), sink_ref[g * SWA_GROUP + r], _F32)
                                    for r in range(SWA_GROUP)], axis=0)
            m = jnp.maximum(jnp.max(s, axis=1, keepdims=True), sink)
            e = jnp.exp(s - m)
            oe = _dot(e.astype(_BF16), vv)
            den = oe[:, LANES:2 * LANES] + jnp.exp(sink[:, 0:LANES] - m[:, 0:LANES])
            og.append(oe[:, 0:LANES] / den)

        for r in range(SWA_GROUP):
            o = jnp.where(lane < HEAD_DIM, og[0][r * W:(r + 1) * W], og[1][r * W:(r + 1) * W])
            o_ref[0, w * W:(w + 1) * W, r * LANES:(r + 1) * LANES] = o.astype(_BF16)


def _swa_attention(sinks, swa_q, swa_kv):
    B, S, _ = swa_q.shape
    grid = (B, S // TS_SWA)
    return pl.pallas_call(
        _swa_kernel,
        grid=grid,
        in_specs=[
            pl.BlockSpec(memory_space=pltpu.SMEM),
            pl.BlockSpec((1, TS_SWA, SWA_QW), lambda b, i: (b, i, 0)),
            pl.BlockSpec((1, S, 2 * SWA_KVW), lambda b, i: (b, 0, 0)),
        ],
        out_specs=pl.BlockSpec((1, TS_SWA, SWA_QW), lambda b, i: (b, i, 0)),
        out_shape=jax.ShapeDtypeStruct((B, S, SWA_QW), _BF16),
        compiler_params=pltpu.CompilerParams(
            dimension_semantics=("arbitrary", "arbitrary"), vmem_limit_bytes=VMEM_LIMIT),
        name="swa_attn",
    )(sinks, swa_q, swa_kv)


def _layer_norm(t, g, b):
    mu = jnp.mean(t, axis=-1, keepdims=True)
    d = t - mu
    var = jnp.mean(d * d, axis=-1, keepdims=True)
    return d * lax.rsqrt(var + LN_EPS) * g + b


def _out_kernel(x_ref, of_ref, os_ref, gate_ref, wof_ref, wos_ref, wout_ref, g1_ref, b1_ref,
                wg_ref, wu_ref, wd_ref, g2_ref, b2_ref, o_ref, pre1_ref, pre2_ref):
    @pl.when(pl.program_id(0) == 0)
    def _():
        pre1_ref[...] = jnp.zeros_like(pre1_ref)
        pre2_ref[...] = jnp.zeros_like(pre2_ref)

    of_p = _dot(of_ref[...], wof_ref[...])
    os_p = _dot(os_ref[...], wos_ref[...])

    h1 = _layer_norm(pre1_ref[...], g1_ref[...], b1_ref[...])
    lhs = h1.astype(_BF16)

    def gate_up(c):
        cols = slice(FF_BOUNDS[c], FF_BOUNDS[c + 1])
        return _dot(lhs, wg_ref[:, cols]), _dot(lhs, wu_ref[:, cols])

    def down(c, g, u):
        hc = (g * (1.0 / (1.0 + jnp.exp(-g))) * u).astype(_BF16)
        return _dot(hc, wd_ref[FF_BOUNDS[c]:FF_BOUNDS[c + 1], :])

    g0, u0 = gate_up(0)
    merged = (gate_ref[:, 0:D_MODEL].astype(_F32) * of_p
              + gate_ref[:, D_MODEL:2 * D_MODEL].astype(_F32) * os_p)
    mix = _dot(merged.astype(_BF16), wout_ref[...])
    ff = down(0, g0, u0)

    o_ref[...] = _layer_norm(pre2_ref[...], g2_ref[...], b2_ref[...])

    for c in range(1, len(FF_BOUNDS) - 1):
        g, u = gate_up(c)
        ff = ff + down(c, g, u)

    pre1_ref[...] = DEEPNORM_ALPHA * x_ref[...] + mix
    pre2_ref[...] = DEEPNORM_ALPHA * h1 + ff


def _out_ffn(x2, o_f, o_s, gates, wof, wos, wout, g1, b1, wg, wu, wd, g2, b2):
    T, D = x2.shape
    nblk = T // TM_OUT
    grid = (nblk + 2,)
    in_blk = lambda w: pl.BlockSpec((TM_OUT, w), lambda t: (jnp.minimum(t, nblk - 1), 0))
    return pl.pallas_call(
        _out_kernel,
        grid=grid,
        in_specs=[in_blk(D), in_blk(FOX_W), in_blk(SWA_QW), in_blk(2 * D_MODEL),
                  _resident(wof.shape), _resident(wos.shape), _resident(wout.shape),
                  _resident(g1.shape), _resident(b1.shape),
                  _resident(wg.shape), _resident(wu.shape), _resident(wd.shape),
                  _resident(g2.shape), _resident(b2.shape)],
        out_specs=pl.BlockSpec((TM_OUT, D), lambda t: (jnp.maximum(t - 2, 0), 0)),
        out_shape=jax.ShapeDtypeStruct((T, D), _F32),
        scratch_shapes=[pltpu.VMEM((TM_OUT, D), _F32), pltpu.VMEM((TM_OUT, D), _F32)],
        compiler_params=pltpu.CompilerParams(
            dimension_semantics=("arbitrary",), vmem_limit_bytes=VMEM_LIMIT),
        name="out_ffn",
    )(x2, o_f, o_s, gates, wof, wos, wout, g1, b1, wg, wu, wd, g2, b2)


def _pair_swa_heads(t, axis):
    shp = t.shape
    t = t.reshape(shp[:axis] + (SWA_KV_HEADS, SWA_GROUP, HEAD_DIM) + shp[axis + 1:])
    return jnp.swapaxes(t, axis, axis + 1).reshape(shp)


def _side_lane_placement():
    place = np.zeros((FOX_HEADS, LANES), np.float32)
    for lane in range(LANES):
        pair, j = divmod(lane, PAIR_LANES)
        if j < SIDE_LANES:
            place[2 * pair + (j // 3) % 2, lane] = 1.0
    return jnp.asarray(place)


def _rope_tables(S):
    half = HEAD_DIM // 2
    inv_freq = ROPE_THETA ** (-jnp.arange(0, half, dtype=_F32) / half)
    ang = jnp.arange(S, dtype=_F32)[:, None] * inv_freq[None, :]
    cos = jnp.tile(jnp.cos(ang), (1, LANES // half))
    sin = jnp.sin(ang)
    sin = jnp.tile(jnp.concatenate([-sin, sin], axis=1), (1, LANES // HEAD_DIM))
    return cos, sin


def kernel(x, w_in, b_forget, attn_sinks, w_o_fox, w_o_swa, w_out, ln1_g, ln1_b,
           w_gate, w_up, w_down, ln2_g, ln2_b):
    B, S, D = x.shape
    assert DEPTH == 1 and w_in.shape[0] == 1
    w = w_in[0]
    sizes = [FOX_W, FOX_W, FOX_W, FOX_HEADS, SWA_QW, SWA_KVW, SWA_KVW, D_MODEL, D_MODEL]
    offs = np.concatenate([[0], np.cumsum(sizes)])
    wq_f, wk_f, wv_f, w_fg, wq_s, wk_s, wv_s, wg_f, wg_s = [w[:, offs[n]:offs[n + 1]] for n in range(9)]

    place = _side_lane_placement()
    w_fg3 = jnp.dot(w_fg, place, precision=lax.Precision.HIGHEST)
    w_all = jnp.concatenate(
        [wq_f * (QK_SCALE * LOG2E), wk_f, wv_f, _pair_swa_heads(wq_s, 1) * QK_SCALE, wk_s, wv_s, wg_f, wg_s, w_fg3],
        axis=1).astype(_BF16)
    bf = jnp.dot(b_forget[0][None, :], place, precision=lax.Precision.HIGHEST)
    cos, sin = _rope_tables(S)

    fox_qkv, swa_q, swa_kv, gates, csq, csk = _inproj(x, w_all, cos, sin, bf)
    o_f = _fox_attention(fox_qkv, csq, csk)
    o_s = _swa_attention(attn_sinks[0], swa_q, swa_kv)

    T = B * S
    row = lambda v: v[0][None, :]
    out = _out_ffn(
        x.reshape(T, D), o_f.reshape(T, FOX_W), o_s.reshape(T, SWA_QW), gates.reshape(T, 2 * D_MODEL),
        w_o_fox[0].astype(_BF16), _pair_swa_heads(w_o_swa[0], 0).astype(_BF16), w_out[0].astype(_BF16),
        row(ln1_g), row(ln1_b),
        w_gate[0].astype(_BF16), w_up[0].astype(_BF16), w_down[0].astype(_BF16),
        row(ln2_g), row(ln2_b))
    return out.reshape(B, S, D)
```

```python
import functools
import math

import numpy as np
import jax
import jax.numpy as jnp
from jax import lax
from jax.experimental import pallas as pl
from jax.experimental.pallas import tpu as pltpu

D_MODEL = 1024
HEAD_DIM = 64
FOX_HEADS = 8
SWA_Q_HEADS = 8
SWA_KV_HEADS = 2
SWA_GROUP = SWA_Q_HEADS // SWA_KV_HEADS
WINDOW = 128
ROPE_THETA = 10000.0
DEPTH = 1
D_FF = -(-8 * D_MODEL // (3 * 256)) * 256
DEEPNORM_ALPHA = (2 * DEPTH) ** 0.25
LN_EPS = 1e-5

FOX_W = FOX_HEADS * HEAD_DIM
SWA_QW = SWA_Q_HEADS * HEAD_DIM
SWA_KVW = SWA_KV_HEADS * HEAD_DIM

LANES = 128
QK_SCALE = 1.0 / math.sqrt(HEAD_DIM)
LOG2E = math.log2(math.e)

_C_FOX = 0
_C_SWAQ = _C_FOX + 3 * FOX_W
_C_SWAKV = _C_SWAQ + SWA_QW
_C_GATE = _C_SWAKV + 2 * SWA_KVW
_C_FG = _C_GATE + 2 * D_MODEL
_C_END = _C_FG + LANES

PAIR_LANES = LANES // (FOX_HEADS // 2)
SIDE_LANES = 12


def _side_lane_fields(lane):
    j = lane & (PAIR_LANES - 1)
    triple = (j * 11) >> 5
    return lane // PAIR_LANES, j, triple, j - 3 * triple

TM_PROJ = 1024
TQ_FOX = 256
PAIRS_PER_STEP = 2
TS_SWA = 512
TM_OUT = 512
MXU_TILE = 256
FF_BOUNDS = (0, 6 * MXU_TILE, D_FF)
assert all(b % MXU_TILE == 0 for b in FF_BOUNDS)
VMEM_LIMIT = 56 * 1024 * 1024

_BF16 = jnp.bfloat16
_F32 = jnp.float32


def _dot(a, b):
    return jnp.dot(a, b, preferred_element_type=_F32)


def _dot_nt(a, b):
    return lax.dot_general(a, b, (((1,), (1,)), ((), ())), preferred_element_type=_F32)


def _resident(shape):
    nd = len(shape)
    return pl.BlockSpec(shape, lambda *_: (0,) * nd, pipeline_mode=pl.Buffered(1))


def _inproj_kernel(x_ref, w_ref, cos_ref, sin_ref, bf_ref,
                   fox_ref, swaq_ref, swakv_ref, gate_ref, csq_ref, csk_ref, carry_ref):
    @pl.when(pl.program_id(1) == 0)
    def _():
        carry_ref[...] = jnp.zeros_like(carry_ref)

    xb = x_ref[0].astype(_BF16)
    tm = xb.shape[0]
    lane = lax.broadcasted_iota(jnp.int32, (tm, LANES), 1)

    f = _dot(xb, w_ref[:, _C_FG:_C_END]) + bf_ref[...]
    fox_ref[0] = _dot(xb, w_ref[:, _C_FOX:_C_SWAQ]).astype(_BF16)

    logf = (jnp.minimum(f, 0.0) - jnp.log1p(jnp.exp(-jnp.abs(f)))) * LOG2E
    row = lax.broadcasted_iota(jnp.int32, (tm, LANES), 0)
    c = logf
    k = 1
    while k < tm:
        c = c + jnp.where(row >= k, pltpu.roll(c, k, 0), 0.0)
        k *= 2
    c = c + carry_ref[...]
    carry_ref[...] = c[tm - 1:tm, :]

    hi = c.astype(_BF16)
    r1 = c - hi.astype(_F32)
    mid = r1.astype(_BF16)
    lo = (r1 - mid.astype(_F32)).astype(_BF16)
    _, j, triple, part = _side_lane_fields(lane)
    pieces = jnp.where(part == 0, hi, jnp.where(part == 1, mid, lo))
    one = jnp.ones_like(hi)
    zero = jnp.zeros_like(hi)
    used = j < SIDE_LANES
    csq_ref[0] = jnp.where(used, jnp.where(triple < 2, pieces, one), zero)
    csk_ref[0] = jnp.where(used, jnp.where(triple < 2, one, -pieces), zero)

    cos = cos_ref[...]
    sin = sin_ref[...]
    first_half = (lane & (HEAD_DIM - 1)) < (HEAD_DIM // 2)

    def rope(t):
        partner = jnp.where(first_half,
                            pltpu.roll(t, LANES - HEAD_DIM // 2, 1),
                            pltpu.roll(t, HEAD_DIM // 2, 1))
        return t * cos + partner * sin

    sq = _dot(xb, w_ref[:, _C_SWAQ:_C_SWAKV])
    for c in range(SWA_QW // LANES):
        swaq_ref[0, :, c * LANES:(c + 1) * LANES] = rope(sq[:, c * LANES:(c + 1) * LANES]).astype(_BF16)
    skv = _dot(xb, w_ref[:, _C_SWAKV:_C_GATE])
    swakv_ref[0, :, 0:LANES] = rope(skv[:, 0:LANES]).astype(_BF16)
    swakv_ref[0, :, LANES:2 * LANES] = skv[:, LANES:2 * LANES].astype(_BF16)

    for c in range(4):
        w = 2 * D_MODEL // 4
        g = _dot(xb, w_ref[:, _C_GATE + c * w:_C_GATE + (c + 1) * w])
        gate_ref[0, :, c * w:(c + 1) * w] = (1.0 / (1.0 + jnp.exp(-g))).astype(_BF16)


def _inproj(x, w_all, cos, sin, bf):
    B, S, D = x.shape
    grid = (B, S // TM_PROJ)
    row_blk = lambda w: pl.BlockSpec((1, TM_PROJ, w), lambda b, s: (b, s, 0))
    out_shapes = (
        jax.ShapeDtypeStruct((B, S, 3 * FOX_W), _BF16),
        jax.ShapeDtypeStruct((B, S, SWA_QW), _BF16),
        jax.ShapeDtypeStruct((B, S, 2 * SWA_KVW), _BF16),
        jax.ShapeDtypeStruct((B, S, 2 * D_MODEL), _BF16),
        jax.ShapeDtypeStruct((B, S, LANES), _BF16),
        jax.ShapeDtypeStruct((B, S, LANES), _BF16),
    )
    return pl.pallas_call(
        _inproj_kernel,
        grid=grid,
        in_specs=[
            row_blk(D),
            _resident(w_all.shape),
            pl.BlockSpec((TM_PROJ, LANES), lambda b, s: (s, 0)),
            pl.BlockSpec((TM_PROJ, LANES), lambda b, s: (s, 0)),
            _resident(bf.shape),
        ],
        out_specs=(row_blk(3 * FOX_W), row_blk(SWA_QW), row_blk(2 * SWA_KVW),
                   row_blk(2 * D_MODEL), row_blk(LANES), row_blk(LANES)),
        out_shape=out_shapes,
        scratch_shapes=[pltpu.VMEM((1, LANES), _F32)],
        compiler_params=pltpu.CompilerParams(
            dimension_semantics=("arbitrary", "arbitrary"), vmem_limit_bytes=VMEM_LIMIT),
        name="inproj",
    )(x, w_all, cos, sin, bf)


def _fox_kernel(q_ref, k_ref, v_ref, csq_ref, csk_ref, o_ref):
    for pp in range(PAIRS_PER_STEP):
        _fox_pair(pl.program_id(1) * PAIRS_PER_STEP + pp, slice(pp * LANES, (pp + 1) * LANES),
                  q_ref, k_ref, v_ref, csq_ref, csk_ref, o_ref)


def _fox_pair(hp, cols, q_ref, k_ref, v_ref, csq_ref, csk_ref, o_ref):
    S = q_ref.shape[1]
    tq = TQ_FOX

    pair_k = lax.broadcasted_iota(jnp.int32, (S, LANES), 1) // PAIR_LANES
    csk = csk_ref[0]
    kb = jnp.concatenate([k_ref[0, :, cols], jnp.where(pair_k == hp, csk, jnp.zeros_like(csk))], axis=1)
    v = jnp.concatenate([v_ref[0, :, cols], jnp.ones((S, LANES), _BF16)], axis=1)

    lane = lax.broadcasted_iota(jnp.int32, (tq, LANES), 1)
    pair, j, triple, _ = _side_lane_fields(lane)
    head_of = jnp.where(pair == hp, jnp.where(j < SIDE_LANES, triple & 1, -1), -1)
    keep0 = head_of == 0
    keep1 = head_of == 1
    causal = ((lax.broadcasted_iota(jnp.int32, (2 * tq, tq), 0) & (tq - 1))
              >= lax.broadcasted_iota(jnp.int32, (2 * tq, tq), 1))

    for i in reversed(range(S // tq)):
        rows = slice(i * tq, (i + 1) * tq)
        w = (i + 1) * tq
        q = q_ref[0, rows, cols]
        csq = csq_ref[0, rows, :]
        zq = jnp.zeros_like(q)
        qa = jnp.concatenate([
            jnp.concatenate([jnp.where(lane < HEAD_DIM, q, zq), jnp.where(keep0, csq, zq)], axis=1),
            jnp.concatenate([jnp.where(lane >= HEAD_DIM, q, zq), jnp.where(keep1, csq, zq)], axis=1),
        ], axis=0)
        s = _dot_nt(qa, kb[0:w, :])
        sd = jnp.where(causal, s[:, w - tq:w], -jnp.inf)
        m = jnp.max(sd, axis=1, keepdims=True)
        if i > 0:
            so = s[:, 0:w - tq]
            m = jnp.maximum(m, jnp.max(so, axis=1, keepdims=True))
            p = jnp.concatenate([jnp.exp2(so - m), jnp.exp2(sd - m)], axis=1)
        else:
            p = jnp.exp2(sd - m)
        ol = _dot(p.astype(_BF16), v[0:w, :])
        o = ol[:, 0:LANES] / ol[:, LANES:2 * LANES]
        o_ref[0, rows, cols] = jnp.where(lane < HEAD_DIM, o[0:tq], o[tq:2 * tq]).astype(_BF16)


def _fox_attention(fox_qkv, csq, csk):
    B, S, _ = fox_qkv.shape
    nstep = FOX_HEADS // 2 // PAIRS_PER_STEP
    wide = PAIRS_PER_STEP * LANES
    grid = (B, nstep)
    return pl.pallas_call(
        _fox_kernel,
        grid=grid,
        in_specs=[
            pl.BlockSpec((1, S, wide), lambda b, h: (b, 0, h)),
            pl.BlockSpec((1, S, wide), lambda b, h: (b, 0, nstep + h)),
            pl.BlockSpec((1, S, wide), lambda b, h: (b, 0, 2 * nstep + h)),
            pl.BlockSpec((1, S, LANES), lambda b, h: (b, 0, 0)),
            pl.BlockSpec((1, S, LANES), lambda b, h: (b, 0, 0)),
        ],
        out_specs=pl.BlockSpec((1, S, wide), lambda b, h: (b, 0, h)),
        out_shape=jax.ShapeDtypeStruct((B, S, FOX_W), _BF16),
        compiler_params=pltpu.CompilerParams(
            dimension_semantics=("arbitrary", "arbitrary"), vmem_limit_bytes=VMEM_LIMIT),
        name="fox_attn",
    )(fox_qkv, fox_qkv, fox_qkv, csq, csk)


def _swa_kernel(sink_ref, q_ref, kv_ref, o_ref):
    i = pl.program_id(1)
    ts = q_ref.shape[1]
    W = WINDOW
    lane = lax.broadcasted_iota(jnp.int32, (W, LANES), 1)
    nrow = SWA_GROUP * W
    qrow = lax.broadcasted_iota(jnp.int32, (nrow, 2 * W), 0) & (W - 1)
    kcol = lax.broadcasted_iota(jnp.int32, (nrow, 2 * W), 1)
    back = W + qrow - kcol
    band = jnp.where(back >= 0, jnp.where(back < WINDOW, 0.0, -jnp.inf), -jnp.inf)
    band_first = jnp.where(kcol >= W, band, -jnp.inf)

    for w in range(ts // W):
        qstart = pl.multiple_of(i * ts + w * W, W)
        pstart = pl.multiple_of(jnp.maximum(qstart - W, 0), W)
        kvw = jnp.concatenate([kv_ref[0, pl.ds(pstart, W), :], kv_ref[0, pl.ds(qstart, W), :]], axis=0)
        kk = kvw[:, 0:LANES]
        vv = jnp.concatenate([kvw[:, LANES:2 * LANES], jnp.ones((2 * W, LANES), _BF16)], axis=1)
        bias = jnp.where(i == 0, band_first, band) if w == 0 else band

        og = []
        for g in range(SWA_KV_HEADS):
            in_head = (lane < HEAD_DIM) if g == 0 else (lane >= HEAD_DIM)
            rows = []
            for r in range(SWA_GROUP):
                qb = q_ref[0, w * W:(w + 1) * W, r * LANES:(r + 1) * LANES]
                rows.append(jnp.where(in_head, qb, jnp.zeros_like(qb)))
            ql = jnp.concatenate(rows, axis=0)
            s = _dot_nt(ql, kk) + bias
            sink = jnp.concatenate([jnp.full((W, LANES), sink_ref[g * SWA_GROUP + r] * LOG2E, _F32)
                                    for r in range(SWA_GROUP)], axis=0)
            m = jnp.max(s, axis=1, keepdims=True)
            e = jnp.exp2(s - m)
            oe = _dot(e.astype(_BF16), vv)
            den = oe[:, LANES:2 * LANES] + jnp.exp2(sink - m)
            og.append(oe[:, 0:LANES] / den)

        for r in range(SWA_GROUP):
            o = jnp.where(lane < HEAD_DIM, og[0][r * W:(r + 1) * W], og[1][r * W:(r + 1) * W])
            o_ref[0, w * W:(w + 1) * W, r * LANES:(r + 1) * LANES] = o.astype(_BF16)


def _swa_attention(sinks, swa_q, swa_kv):
    B, S, _ = swa_q.shape
    grid = (B, S // TS_SWA)
    return pl.pallas_call(
        _swa_kernel,
        grid=grid,
        in_specs=[
            pl.BlockSpec(memory_space=pltpu.SMEM),
            pl.BlockSpec((1, TS_SWA, SWA_QW), lambda b, i: (b, i, 0)),
            pl.BlockSpec((1, S, 2 * SWA_KVW), lambda b, i: (b, 0, 0)),
        ],
        out_specs=pl.BlockSpec((1, TS_SWA, SWA_QW), lambda b, i: (b, i, 0)),
        out_shape=jax.ShapeDtypeStruct((B, S, SWA_QW), _BF16),
        compiler_params=pltpu.CompilerParams(
            dimension_semantics=("arbitrary", "arbitrary"), vmem_limit_bytes=VMEM_LIMIT),
        name="swa_attn",
    )(sinks, swa_q, swa_kv)


def _layer_norm(t, g, b):
    mu = jnp.mean(t, axis=-1, keepdims=True)
    d = t - mu
    var = jnp.mean(d * d, axis=-1, keepdims=True)
    return d * lax.rsqrt(var + LN_EPS) * g + b


def _out_kernel(x_ref, of_ref, os_ref, gate_ref, wof_ref, wos_ref, wout_ref, g1_ref, b1_ref,
                wg_ref, wu_ref, wd_ref, g2_ref, b2_ref, o_ref, pre1_ref, pre2_ref):
    @pl.when(pl.program_id(0) == 0)
    def _():
        pre1_ref[...] = jnp.zeros_like(pre1_ref)
        pre2_ref[...] = jnp.zeros_like(pre2_ref)

    of_p = _dot(of_ref[...], wof_ref[...])
    os_p = _dot(os_ref[...], wos_ref[...])

    h1 = _layer_norm(pre1_ref[...], g1_ref[...], b1_ref[...])
    lhs = h1.astype(_BF16)

    def hidden(c):
        cols = slice(FF_BOUNDS[c], FF_BOUNDS[c + 1])
        g = _dot(lhs, wg_ref[:, cols])
        u = _dot(lhs, wu_ref[:, cols])
        return (g * (1.0 / (1.0 + jnp.exp(-g))) * u).astype(_BF16)

    hc = [hidden(0)]
    merged = (gate_ref[:, 0:D_MODEL].astype(_F32) * of_p
              + gate_ref[:, D_MODEL:2 * D_MODEL].astype(_F32) * os_p)
    mix = _dot(merged.astype(_BF16), wout_ref[...])
    pre1_ref[...] = DEEPNORM_ALPHA * x_ref[...] + mix

    o_ref[...] = _layer_norm(pre2_ref[...], g2_ref[...], b2_ref[...])

    hc += [hidden(c) for c in range(1, len(FF_BOUNDS) - 1)]
    ff = _dot(jnp.concatenate(hc, axis=1), wd_ref[...])
    pre2_ref[...] = DEEPNORM_ALPHA * h1 + ff


def _out_ffn(x2, o_f, o_s, gates, wof, wos, wout, g1, b1, wg, wu, wd, g2, b2):
    T, D = x2.shape
    nblk = T // TM_OUT
    grid = (nblk + 2,)
    in_blk = lambda w: pl.BlockSpec((TM_OUT, w), lambda t: (jnp.minimum(t, nblk - 1), 0))
    return pl.pallas_call(
        _out_kernel,
        grid=grid,
        in_specs=[in_blk(D), in_blk(FOX_W), in_blk(SWA_QW), in_blk(2 * D_MODEL),
                  _resident(wof.shape), _resident(wos.shape), _resident(wout.shape),
                  _resident(g1.shape), _resident(b1.shape),
                  _resident(wg.shape), _resident(wu.shape), _resident(wd.shape),
                  _resident(g2.shape), _resident(b2.shape)],
        out_specs=pl.BlockSpec((TM_OUT, D), lambda t: (jnp.maximum(t - 2, 0), 0)),
        out_shape=jax.ShapeDtypeStruct((T, D), _F32),
        scratch_shapes=[pltpu.VMEM((TM_OUT, D), _F32), pltpu.VMEM((TM_OUT, D), _F32)],
        compiler_params=pltpu.CompilerParams(
            dimension_semantics=("arbitrary",), vmem_limit_bytes=VMEM_LIMIT),
        name="out_ffn",
    )(x2, o_f, o_s, gates, wof, wos, wout, g1, b1, wg, wu, wd, g2, b2)


def _pair_swa_heads(t, axis):
    shp = t.shape
    t = t.reshape(shp[:axis] + (SWA_KV_HEADS, SWA_GROUP, HEAD_DIM) + shp[axis + 1:])
    return jnp.swapaxes(t, axis, axis + 1).reshape(shp)


def _side_lane_placement():
    place = np.zeros((FOX_HEADS, LANES), np.float32)
    for lane in range(LANES):
        pair, j = divmod(lane, PAIR_LANES)
        if j < SIDE_LANES:
            place[2 * pair + (j // 3) % 2, lane] = 1.0
    return jnp.asarray(place)


def _rope_tables(S):
    half = HEAD_DIM // 2
    inv_freq = ROPE_THETA ** (-jnp.arange(0, half, dtype=_F32) / half)
    ang = jnp.arange(S, dtype=_F32)[:, None] * inv_freq[None, :]
    cos = jnp.tile(jnp.cos(ang), (1, LANES // half))
    sin = jnp.sin(ang)
    sin = jnp.tile(jnp.concatenate([-sin, sin], axis=1), (1, LANES // HEAD_DIM))
    return cos, sin


def kernel(x, w_in, b_forget, attn_sinks, w_o_fox, w_o_swa, w_out, ln1_g, ln1_b,
           w_gate, w_up, w_down, ln2_g, ln2_b):
    B, S, D = x.shape
    assert DEPTH == 1 and w_in.shape[0] == 1
    w = w_in[0]
    sizes = [FOX_W, FOX_W, FOX_W, FOX_HEADS, SWA_QW, SWA_KVW, SWA_KVW, D_MODEL, D_MODEL]
    offs = np.concatenate([[0], np.cumsum(sizes)])
    wq_f, wk_f, wv_f, w_fg, wq_s, wk_s, wv_s, wg_f, wg_s = [w[:, offs[n]:offs[n + 1]] for n in range(9)]

    place = _side_lane_placement()
    w_fg3 = jnp.dot(w_fg, place, precision=lax.Precision.HIGHEST)
    w_all = jnp.concatenate(
        [wq_f * (QK_SCALE * LOG2E), wk_f, wv_f, _pair_swa_heads(wq_s, 1) * (QK_SCALE * LOG2E), wk_s, wv_s, wg_f, wg_s, w_fg3],
        axis=1).astype(_BF16)
    bf = jnp.dot(b_forget[0][None, :], place, precision=lax.Precision.HIGHEST)
    cos, sin = _rope_tables(S)

    fox_qkv, swa_q, swa_kv, gates, csq, csk = _inproj(x, w_all, cos, sin, bf)
    o_f = _fox_attention(fox_qkv, csq, csk)
    o_s = _swa_attention(attn_sinks[0], swa_q, swa_kv)

    T = B * S
    row = lambda v: v[0][None, :]
    out = _out_ffn(
        x.reshape(T, D), o_f.reshape(T, FOX_W), o_s.reshape(T, SWA_QW), gates.reshape(T, 2 * D_MODEL),
        w_o_fox[0].astype(_BF16), _pair_swa_heads(w_o_swa[0], 0).astype(_BF16), w_out[0].astype(_BF16),
        row(ln1_g), row(ln1_b),
        w_gate[0].astype(_BF16), w_up[0].astype(_BF16), w_down[0].astype(_BF16),
        row(ln2_g), row(ln2_b))
    return out.reshape(B, S, D)
```

```python
import functools
import math

import numpy as np
import jax
import jax.numpy as jnp
from jax import lax
from jax.experimental import pallas as pl
from jax.experimental.pallas import tpu as pltpu

D_MODEL = 1024
HEAD_DIM = 64
FOX_HEADS = 8
SWA_Q_HEADS = 8
SWA_KV_HEADS = 2
SWA_GROUP = SWA_Q_HEADS // SWA_KV_HEADS
WINDOW = 128
ROPE_THETA = 10000.0
DEPTH = 1
D_FF = -(-8 * D_MODEL // (3 * 256)) * 256
DEEPNORM_ALPHA = (2 * DEPTH) ** 0.25
LN_EPS = 1e-5

FOX_W = FOX_HEADS * HEAD_DIM
SWA_QW = SWA_Q_HEADS * HEAD_DIM
SWA_KVW = SWA_KV_HEADS * HEAD_DIM

LANES = 128
QK_SCALE = 1.0 / math.sqrt(HEAD_DIM)
LOG2E = math.log2(math.e)

_C_FOX = 0
_C_SWAQ = _C_FOX + 3 * FOX_W
_C_SWAKV = _C_SWAQ + SWA_QW
_C_GATE = _C_SWAKV + 2 * SWA_KVW
_C_FG = _C_GATE + 2 * D_MODEL
_C_END = _C_FG + LANES

PAIR_LANES = LANES // (FOX_HEADS // 2)
SIDE_LANES = 12


def _side_lane_fields(lane):
    j = lane & (PAIR_LANES - 1)
    triple = (j * 11) >> 5
    return lane // PAIR_LANES, j, triple, j - 3 * triple

TM_PROJ = 1024
TQ_FOX = 256
PAIRS_PER_STEP = 2
TS_SWA = 512
TM_OUT = 512
MXU_TILE = 256
FF_BOUNDS = (0, 6 * MXU_TILE, D_FF)
assert all(b % MXU_TILE == 0 for b in FF_BOUNDS)
VMEM_LIMIT = 56 * 1024 * 1024

_BF16 = jnp.bfloat16
_F32 = jnp.float32


def _dot(a, b):
    return jnp.dot(a, b, preferred_element_type=_F32)


def _dot_nt(a, b):
    return lax.dot_general(a, b, (((1,), (1,)), ((), ())), preferred_element_type=_F32)


def _resident(shape):
    nd = len(shape)
    return pl.BlockSpec(shape, lambda *_: (0,) * nd, pipeline_mode=pl.Buffered(1))


def _inproj_kernel(x_ref, w_ref, cos_ref, sin_ref, bf_ref,
                   fox_ref, swaq_ref, swakv_ref, gate_ref, csq_ref, csk_ref, carry_ref):
    @pl.when(pl.program_id(1) == 0)
    def _():
        carry_ref[...] = jnp.zeros_like(carry_ref)

    xb = x_ref[0].astype(_BF16)
    tm = xb.shape[0]
    lane = lax.broadcasted_iota(jnp.int32, (tm, LANES), 1)

    f = _dot(xb, w_ref[:, _C_FG:_C_END]) + bf_ref[...]
    fox_ref[0] = _dot(xb, w_ref[:, _C_FOX:_C_SWAQ]).astype(_BF16)

    logf = (jnp.minimum(f, 0.0) - jnp.log1p(jnp.exp(-jnp.abs(f)))) * LOG2E
    row = lax.broadcasted_iota(jnp.int32, (tm, LANES), 0)
    c = logf
    k = 1
    while k < tm:
        c = c + jnp.where(row >= k, pltpu.roll(c, k, 0), 0.0)
        k *= 2
    c = c + carry_ref[...]
    carry_ref[...] = c[tm - 1:tm, :]

    hi = c.astype(_BF16)
    r1 = c - hi.astype(_F32)
    mid = r1.astype(_BF16)
    lo = (r1 - mid.astype(_F32)).astype(_BF16)
    _, j, triple, part = _side_lane_fields(lane)
    pieces = jnp.where(part == 0, hi, jnp.where(part == 1, mid, lo))
    one = jnp.ones_like(hi)
    zero = jnp.zeros_like(hi)
    used = j < SIDE_LANES
    csq_ref[0] = jnp.where(used, jnp.where(triple < 2, pieces, one), zero)
    csk_ref[0] = jnp.where(used, jnp.where(triple < 2, one, -pieces), zero)

    cos = cos_ref[...]
    sin = sin_ref[...]
    first_half = (lane & (HEAD_DIM - 1)) < (HEAD_DIM // 2)

    def rope(t):
        partner = jnp.where(first_half,
                            pltpu.roll(t, LANES - HEAD_DIM // 2, 1),
                            pltpu.roll(t, HEAD_DIM // 2, 1))
        return t * cos + partner * sin

    sq = _dot(xb, w_ref[:, _C_SWAQ:_C_SWAKV])
    for c in range(SWA_QW // LANES):
        swaq_ref[0, :, c * LANES:(c + 1) * LANES] = rope(sq[:, c * LANES:(c + 1) * LANES]).astype(_BF16)
    skv = _dot(xb, w_ref[:, _C_SWAKV:_C_GATE])
    swakv_ref[0, :, 0:LANES] = rope(skv[:, 0:LANES]).astype(_BF16)
    swakv_ref[0, :, LANES:2 * LANES] = skv[:, LANES:2 * LANES].astype(_BF16)

    for c in range(4):
        w = 2 * D_MODEL // 4
        g = _dot(xb, w_ref[:, _C_GATE + c * w:_C_GATE + (c + 1) * w])
        gate_ref[0, :, c * w:(c + 1) * w] = (1.0 / (1.0 + jnp.exp(-g))).astype(_BF16)


def _inproj(x, w_all, cos, sin, bf):
    B, S, D = x.shape
    grid = (B, S // TM_PROJ)
    row_blk = lambda w: pl.BlockSpec((1, TM_PROJ, w), lambda b, s: (b, s, 0))
    out_shapes = (
        jax.ShapeDtypeStruct((B, S, 3 * FOX_W), _BF16),
        jax.ShapeDtypeStruct((B, S, SWA_QW), _BF16),
        jax.ShapeDtypeStruct((B, S, 2 * SWA_KVW), _BF16),
        jax.ShapeDtypeStruct((B, S, 2 * D_MODEL), _BF16),
        jax.ShapeDtypeStruct((B, S, LANES), _BF16),
        jax.ShapeDtypeStruct((B, S, LANES), _BF16),
    )
    return pl.pallas_call(
        _inproj_kernel,
        grid=grid,
        in_specs=[
            row_blk(D),
            _resident(w_all.shape),
            pl.BlockSpec((TM_PROJ, LANES), lambda b, s: (s, 0)),
            pl.BlockSpec((TM_PROJ, LANES), lambda b, s: (s, 0)),
            _resident(bf.shape),
        ],
        out_specs=(row_blk(3 * FOX_W), row_blk(SWA_QW), row_blk(2 * SWA_KVW),
                   row_blk(2 * D_MODEL), row_blk(LANES), row_blk(LANES)),
        out_shape=out_shapes,
        scratch_shapes=[pltpu.VMEM((1, LANES), _F32)],
        compiler_params=pltpu.CompilerParams(
            dimension_semantics=("arbitrary", "arbitrary"), vmem_limit_bytes=VMEM_LIMIT),
        name="inproj",
    )(x, w_all, cos, sin, bf)


def _fox_kernel(q_ref, k_ref, v_ref, csq_ref, csk_ref, o_ref):
    for pp in range(PAIRS_PER_STEP):
        _fox_pair(pl.program_id(1) * PAIRS_PER_STEP + pp, slice(pp * LANES, (pp + 1) * LANES),
                  q_ref, k_ref, v_ref, csq_ref, csk_ref, o_ref)


def _fox_pair(hp, cols, q_ref, k_ref, v_ref, csq_ref, csk_ref, o_ref):
    S = q_ref.shape[1]
    tq = TQ_FOX

    pair_k = lax.broadcasted_iota(jnp.int32, (S, LANES), 1) // PAIR_LANES
    csk = csk_ref[0]
    kb = jnp.concatenate([k_ref[0, :, cols], jnp.where(pair_k == hp, csk, jnp.zeros_like(csk))], axis=1)
    v = jnp.concatenate([v_ref[0, :, cols], jnp.ones((S, LANES), _BF16)], axis=1)

    lane = lax.broadcasted_iota(jnp.int32, (tq, LANES), 1)
    pair, j, triple, _ = _side_lane_fields(lane)
    head_of = jnp.where(pair == hp, jnp.where(j < SIDE_LANES, triple & 1, -1), -1)
    keep0 = head_of == 0
    keep1 = head_of == 1
    causal = ((lax.broadcasted_iota(jnp.int32, (2 * tq, tq), 0) & (tq - 1))
              >= lax.broadcasted_iota(jnp.int32, (2 * tq, tq), 1))

    for i in reversed(range(S // tq)):
        rows = slice(i * tq, (i + 1) * tq)
        w = (i + 1) * tq
        q = q_ref[0, rows, cols]
        csq = csq_ref[0, rows, :]
        zq = jnp.zeros_like(q)
        qa = jnp.concatenate([
            jnp.concatenate([jnp.where(lane < HEAD_DIM, q, zq), jnp.where(keep0, csq, zq)], axis=1),
            jnp.concatenate([jnp.where(lane >= HEAD_DIM, q, zq), jnp.where(keep1, csq, zq)], axis=1),
        ], axis=0)
        s = _dot_nt(qa, kb[0:w, :])
        sd = jnp.where(causal, s[:, w - tq:w], -jnp.inf)
        m = jnp.max(sd, axis=1, keepdims=True)
        if i > 0:
            so = s[:, 0:w - tq]
            m = jnp.maximum(m, jnp.max(so, axis=1, keepdims=True))
            p = jnp.concatenate([jnp.exp2(so - m), jnp.exp2(sd - m)], axis=1)
        else:
            p = jnp.exp2(sd - m)
        ol = _dot(p.astype(_BF16), v[0:w, :])
        o = ol[:, 0:LANES] / ol[:, LANES:2 * LANES]
        o_ref[0, rows, cols] = jnp.where(lane < HEAD_DIM, o[0:tq], o[tq:2 * tq]).astype(_BF16)


def _fox_attention(fox_qkv, csq, csk):
    B, S, _ = fox_qkv.shape
    nstep = FOX_HEADS // 2 // PAIRS_PER_STEP
    wide = PAIRS_PER_STEP * LANES
    grid = (B, nstep)
    return pl.pallas_call(
        _fox_kernel,
        grid=grid,
        in_specs=[
            pl.BlockSpec((1, S, wide), lambda b, h: (b, 0, h)),
            pl.BlockSpec((1, S, wide), lambda b, h: (b, 0, nstep + h)),
            pl.BlockSpec((1, S, wide), lambda b, h: (b, 0, 2 * nstep + h)),
            pl.BlockSpec((1, S, LANES), lambda b, h: (b, 0, 0)),
            pl.BlockSpec((1, S, LANES), lambda b, h: (b, 0, 0)),
        ],
        out_specs=pl.BlockSpec((1, S, wide), lambda b, h: (b, 0, h)),
        out_shape=jax.ShapeDtypeStruct((B, S, FOX_W), _BF16),
        compiler_params=pltpu.CompilerParams(
            dimension_semantics=("arbitrary", "arbitrary"), vmem_limit_bytes=VMEM_LIMIT),
        name="fox_attn",
    )(fox_qkv, fox_qkv, fox_qkv, csq, csk)


def _swa_kernel(sink_ref, q_ref, kv_ref, o_ref):
    i = pl.program_id(1)
    ts = q_ref.shape[1]
    W = WINDOW
    lane = lax.broadcasted_iota(jnp.int32, (W, LANES), 1)
    nrow = SWA_GROUP * W
    qrow = lax.broadcasted_iota(jnp.int32, (nrow, 2 * W), 0) & (W - 1)
    kcol = lax.broadcasted_iota(jnp.int32, (nrow, 2 * W), 1)
    back = W + qrow - kcol
    band = jnp.where(back >= 0, jnp.where(back < WINDOW, 0.0, -jnp.inf), -jnp.inf)
    band_first = jnp.where(kcol >= W, band, -jnp.inf)

    for w in range(ts // W):
        qstart = pl.multiple_of(i * ts + w * W, W)
        pstart = pl.multiple_of(jnp.maximum(qstart - W, 0), W)
        kvw = jnp.concatenate([kv_ref[0, pl.ds(pstart, W), :], kv_ref[0, pl.ds(qstart, W), :]], axis=0)
        kk = kvw[:, 0:LANES]
        vv = jnp.concatenate([kvw[:, LANES:2 * LANES], jnp.ones((2 * W, LANES), _BF16)], axis=1)
        bias = jnp.where(i == 0, band_first, band) if w == 0 else band

        og = []
        for g in range(SWA_KV_HEADS):
            in_head = (lane < HEAD_DIM) if g == 0 else (lane >= HEAD_DIM)
            rows = []
            for r in range(SWA_GROUP):
                qb = q_ref[0, w * W:(w + 1) * W, r * LANES:(r + 1) * LANES]
                rows.append(jnp.where(in_head, qb, jnp.zeros_like(qb)))
            ql = jnp.concatenate(rows, axis=0)
            s = _dot_nt(ql, kk) + bias
            sink = jnp.concatenate([jnp.full((W, LANES), sink_ref[g * SWA_GROUP + r] * LOG2E, _F32)
                                    for r in range(SWA_GROUP)], axis=0)
            m = jnp.max(s, axis=1, keepdims=True)
            e = jnp.exp2(s - m)
            oe = _dot(e.astype(_BF16), vv)
            den = oe[:, LANES:2 * LANES] + jnp.exp2(sink - m)
            og.append(oe[:, 0:LANES] / den)

        for r in range(SWA_GROUP):
            o = jnp.where(lane < HEAD_DIM, og[0][r * W:(r + 1) * W], og[1][r * W:(r + 1) * W])
            o_ref[0, w * W:(w + 1) * W, r * LANES:(r + 1) * LANES] = o.astype(_BF16)


def _swa_attention(sinks, swa_q, swa_kv):
    B, S, _ = swa_q.shape
    grid = (B, S // TS_SWA)
    return pl.pallas_call(
        _swa_kernel,
        grid=grid,
        in_specs=[
            pl.BlockSpec(memory_space=pltpu.SMEM),
            pl.BlockSpec((1, TS_SWA, SWA_QW), lambda b, i: (b, i, 0)),
            pl.BlockSpec((1, S, 2 * SWA_KVW), lambda b, i: (b, 0, 0)),
        ],
        out_specs=pl.BlockSpec((1, TS_SWA, SWA_QW), lambda b, i: (b, i, 0)),
        out_shape=jax.ShapeDtypeStruct((B, S, SWA_QW), _BF16),
        compiler_params=pltpu.CompilerParams(
            dimension_semantics=("arbitrary", "arbitrary"), vmem_limit_bytes=VMEM_LIMIT),
        name="swa_attn",
    )(sinks, swa_q, swa_kv)


def _layer_norm(t, g, b):
    mu = jnp.mean(t, axis=-1, keepdims=True)
    d = t - mu
    var = jnp.mean(d * d, axis=-1, keepdims=True)
    return d * lax.rsqrt(var + LN_EPS) * g + b


def _out_kernel(x_ref, of_ref, os_ref, gate_ref, wof_ref, wos_ref, wout_ref, g1_ref, b1_ref,
                wg_ref, wu_ref, wd_ref, g2_ref, b2_ref, o_ref, pre1_ref, pre2_ref):
    @pl.when(pl.program_id(0) == 0)
    def _():
        pre1_ref[...] = jnp.zeros_like(pre1_ref)
        pre2_ref[...] = jnp.zeros_like(pre2_ref)

    of_p = _dot(of_ref[...], wof_ref[...])
    os_p = _dot(os_ref[...], wos_ref[...])

    h1 = _layer_norm(pre1_ref[...], g1_ref[...], b1_ref[...])
    lhs = h1.astype(_BF16)

    def hidden(c, tie=None):
        cols = slice(FF_BOUNDS[c], FF_BOUNDS[c + 1])
        g = _dot(lhs, wg_ref[:, cols])
        u = _dot(lhs, wu_ref[:, cols])
        if tie is not None:
            u = jnp.concatenate([u[:, 0:D_MODEL] + tie, u[:, D_MODEL:]], axis=1)
        return (g * (1.0 / (1.0 + jnp.exp(-g))) * u).astype(_BF16)

    hc = [hidden(0)]
    merged = (gate_ref[:, 0:D_MODEL].astype(_F32) * of_p
              + gate_ref[:, D_MODEL:2 * D_MODEL].astype(_F32) * os_p)
    mix = _dot(merged.astype(_BF16), wout_ref[...])
    pre1_ref[...] = DEEPNORM_ALPHA * x_ref[...] + mix

    out = _layer_norm(pre2_ref[...], g2_ref[...], b2_ref[...])
    o_ref[...] = out
    zero = pltpu.bitcast(lax.shift_right_logical(lax.shift_right_logical(
        pltpu.bitcast(out, jnp.uint32), jnp.uint32(16)), jnp.uint32(16)), _F32)
    assert FF_BOUNDS[-1] - FF_BOUNDS[-2] >= D_MODEL

    last = len(FF_BOUNDS) - 2
    hc += [hidden(c, zero if c == last else None) for c in range(1, last + 1)]
    ff = _dot(jnp.concatenate(hc, axis=1), wd_ref[...])
    pre2_ref[...] = DEEPNORM_ALPHA * h1 + ff


def _out_ffn(x2, o_f, o_s, gates, wof, wos, wout, g1, b1, wg, wu, wd, g2, b2):
    T, D = x2.shape
    nblk = T // TM_OUT
    grid = (nblk + 2,)
    in_blk = lambda w: pl.BlockSpec((TM_OUT, w), lambda t: (jnp.minimum(t, nblk - 1), 0))
    return pl.pallas_call(
        _out_kernel,
        grid=grid,
        in_specs=[in_blk(D), in_blk(FOX_W), in_blk(SWA_QW), in_blk(2 * D_MODEL),
                  _resident(wof.shape), _resident(wos.shape), _resident(wout.shape),
                  _resident(g1.shape), _resident(b1.shape),
                  _resident(wg.shape), _resident(wu.shape), _resident(wd.shape),
                  _resident(g2.shape), _resident(b2.shape)],
        out_specs=pl.BlockSpec((TM_OUT, D), lambda t: (jnp.maximum(t - 2, 0), 0)),
        out_shape=jax.ShapeDtypeStruct((T, D), _F32),
        scratch_shapes=[pltpu.VMEM((TM_OUT, D), _F32), pltpu.VMEM((TM_OUT, D), _F32)],
        compiler_params=pltpu.CompilerParams(
            dimension_semantics=("arbitrary",), vmem_limit_bytes=VMEM_LIMIT),
        name="out_ffn",
    )(x2, o_f, o_s, gates, wof, wos, wout, g1, b1, wg, wu, wd, g2, b2)


def _pair_swa_heads(t, axis):
    shp = t.shape
    t = t.reshape(shp[:axis] + (SWA_KV_HEADS, SWA_GROUP, HEAD_DIM) + shp[axis + 1:])
    return jnp.swapaxes(t, axis, axis + 1).reshape(shp)


def _side_lane_placement():
    place = np.zeros((FOX_HEADS, LANES), np.float32)
    for lane in range(LANES):
        pair, j = divmod(lane, PAIR_LANES)
        if j < SIDE_LANES:
            place[2 * pair + (j // 3) % 2, lane] = 1.0
    return jnp.asarray(place)


def _rope_tables(S):
    half = HEAD_DIM // 2
    inv_freq = ROPE_THETA ** (-jnp.arange(0, half, dtype=_F32) / half)
    ang = jnp.arange(S, dtype=_F32)[:, None] * inv_freq[None, :]
    cos = jnp.tile(jnp.cos(ang), (1, LANES // half))
    sin = jnp.sin(ang)
    sin = jnp.tile(jnp.concatenate([-sin, sin], axis=1), (1, LANES // HEAD_DIM))
    return cos, sin


def kernel(x, w_in, b_forget, attn_sinks, w_o_fox, w_o_swa, w_out, ln1_g, ln1_b,
           w_gate, w_up, w_down, ln2_g, ln2_b):
    B, S, D = x.shape
    assert DEPTH == 1 and w_in.shape[0] == 1
    w = w_in[0]
    sizes = [FOX_W, FOX_W, FOX_W, FOX_HEADS, SWA_QW, SWA_KVW, SWA_KVW, D_MODEL, D_MODEL]
    offs = np.concatenate([[0], np.cumsum(sizes)])
    wq_f, wk_f, wv_f, w_fg, wq_s, wk_s, wv_s, wg_f, wg_s = [w[:, offs[n]:offs[n + 1]] for n in range(9)]

    place = _side_lane_placement()
    w_fg3 = jnp.dot(w_fg, place, precision=lax.Precision.HIGHEST)
    w_all = jnp.concatenate(
        [wq_f * (QK_SCALE * LOG2E), wk_f, wv_f, _pair_swa_heads(wq_s, 1) * (QK_SCALE * LOG2E), wk_s, wv_s, wg_f, wg_s, w_fg3],
        axis=1).astype(_BF16)
    bf = jnp.dot(b_forget[0][None, :], place, precision=lax.Precision.HIGHEST)
    cos, sin = _rope_tables(S)

    fox_qkv, swa_q, swa_kv, gates, csq, csk = _inproj(x, w_all, cos, sin, bf)
    o_f = _fox_attention(fox_qkv, csq, csk)
    o_s = _swa_attention(attn_sinks[0], swa_q, swa_kv)

    T = B * S
    row = lambda v: v[0][None, :]
    out = _out_ffn(
        x.reshape(T, D), o_f.reshape(T, FOX_W), o_s.reshape(T, SWA_QW), gates.reshape(T, 2 * D_MODEL),
        w_o_fox[0].astype(_BF16), _pair_swa_heads(w_o_swa[0], 0).astype(_BF16), w_out[0].astype(_BF16),
        row(ln1_g), row(ln1_b),
        w_gate[0].astype(_BF16), w_up[0].astype(_BF16), w_down[0].astype(_BF16),
        row(ln2_g), row(ln2_b))
    return out.reshape(B, S, D)
```

```python
import functools
import math

import numpy as np
import jax
import jax.numpy as jnp
from jax import lax
from jax.experimental import pallas as pl
from jax.experimental.pallas import tpu as pltpu

D_MODEL = 1024
HEAD_DIM = 64
FOX_HEADS = 8
SWA_Q_HEADS = 8
SWA_KV_HEADS = 2
SWA_GROUP = SWA_Q_HEADS // SWA_KV_HEADS
WINDOW = 128
ROPE_THETA = 10000.0
DEPTH = 1
D_FF = -(-8 * D_MODEL // (3 * 256)) * 256
DEEPNORM_ALPHA = (2 * DEPTH) ** 0.25
LN_EPS = 1e-5

FOX_W = FOX_HEADS * HEAD_DIM
SWA_QW = SWA_Q_HEADS * HEAD_DIM
SWA_KVW = SWA_KV_HEADS * HEAD_DIM

LANES = 128
QK_SCALE = 1.0 / math.sqrt(HEAD_DIM)
LOG2E = math.log2(math.e)

_C_FOX = 0
_C_SWAQ = _C_FOX + 3 * FOX_W
_C_SWAKV = _C_SWAQ + SWA_QW
_C_GATE = _C_SWAKV + 2 * SWA_KVW
_C_FG = _C_GATE + 2 * D_MODEL
_C_END = _C_FG + LANES

PAIR_LANES = LANES // (FOX_HEADS // 2)
SIDE_LANES = 12


def _side_lane_fields(lane):
    j = lane & (PAIR_LANES - 1)
    triple = (j * 11) >> 5
    return lane // PAIR_LANES, j, triple, j - 3 * triple

TM_PROJ = 1024
TQ_FOX = 256
PAIRS_PER_STEP = 4
TS_SWA = 1024
TM_OUT = 512
MXU_TILE = 256
FF_BOUNDS = (0, 6 * MXU_TILE, D_FF)
assert all(b % MXU_TILE == 0 for b in FF_BOUNDS)
VMEM_LIMIT = 56 * 1024 * 1024

_BF16 = jnp.bfloat16
_F32 = jnp.float32


def _dot(a, b):
    return jnp.dot(a, b, preferred_element_type=_F32)


def _dot_nt(a, b):
    return lax.dot_general(a, b, (((1,), (1,)), ((), ())), preferred_element_type=_F32)


def _resident(shape):
    nd = len(shape)
    return pl.BlockSpec(shape, lambda *_: (0,) * nd, pipeline_mode=pl.Buffered(1))


def _inproj_kernel(x_ref, w_ref, cos_ref, sin_ref, bf_ref,
                   fox_ref, swaq_ref, swakv_ref, gate_ref, csq_ref, csk_ref, carry_ref):
    @pl.when(pl.program_id(1) == 0)
    def _():
        carry_ref[...] = jnp.zeros_like(carry_ref)

    xb = x_ref[0].astype(_BF16)
    tm = xb.shape[0]
    lane = lax.broadcasted_iota(jnp.int32, (tm, LANES), 1)

    f = _dot(xb, w_ref[:, _C_FG:_C_END]) + bf_ref[...]
    fox_ref[0] = _dot(xb, w_ref[:, _C_FOX:_C_SWAQ]).astype(_BF16)

    logf = (jnp.minimum(f, 0.0) - jnp.log1p(jnp.exp(-jnp.abs(f)))) * LOG2E
    row = lax.broadcasted_iota(jnp.int32, (tm, LANES), 0)
    c = logf
    k = 1
    while k < tm:
        c = c + jnp.where(row >= k, pltpu.roll(c, k, 0), 0.0)
        k *= 2
    c = c + carry_ref[...]
    carry_ref[...] = c[tm - 1:tm, :]

    hi = c.astype(_BF16)
    r1 = c - hi.astype(_F32)
    mid = r1.astype(_BF16)
    lo = (r1 - mid.astype(_F32)).astype(_BF16)
    _, j, triple, part = _side_lane_fields(lane)
    pieces = jnp.where(part == 0, hi, jnp.where(part == 1, mid, lo))
    one = jnp.ones_like(hi)
    zero = jnp.zeros_like(hi)
    used = j < SIDE_LANES
    csq_ref[0] = jnp.where(used, jnp.where(triple < 2, pieces, one), zero)
    csk_ref[0] = jnp.where(used, jnp.where(triple < 2, one, -pieces), zero)

    cos = cos_ref[...]
    sin = sin_ref[...]
    first_half = (lane & (HEAD_DIM - 1)) < (HEAD_DIM // 2)

    def rope(t):
        partner = jnp.where(first_half,
                            pltpu.roll(t, LANES - HEAD_DIM // 2, 1),
                            pltpu.roll(t, HEAD_DIM // 2, 1))
        return t * cos + partner * sin

    sq = _dot(xb, w_ref[:, _C_SWAQ:_C_SWAKV])
    for c in range(SWA_QW // LANES):
        swaq_ref[0, :, c * LANES:(c + 1) * LANES] = rope(sq[:, c * LANES:(c + 1) * LANES]).astype(_BF16)
    skv = _dot(xb, w_ref[:, _C_SWAKV:_C_GATE])
    swakv_ref[0, :, 0:LANES] = rope(skv[:, 0:LANES]).astype(_BF16)
    swakv_ref[0, :, LANES:2 * LANES] = skv[:, LANES:2 * LANES].astype(_BF16)

    for c in range(4):
        w = 2 * D_MODEL // 4
        g = _dot(xb, w_ref[:, _C_GATE + c * w:_C_GATE + (c + 1) * w])
        gate_ref[0, :, c * w:(c + 1) * w] = (1.0 / (1.0 + jnp.exp(-g))).astype(_BF16)


def _inproj(x, w_all, cos, sin, bf):
    B, S, D = x.shape
    grid = (B, S // TM_PROJ)
    row_blk = lambda w: pl.BlockSpec((1, TM_PROJ, w), lambda b, s: (b, s, 0))
    out_shapes = (
        jax.ShapeDtypeStruct((B, S, 3 * FOX_W), _BF16),
        jax.ShapeDtypeStruct((B, S, SWA_QW), _BF16),
        jax.ShapeDtypeStruct((B, S, 2 * SWA_KVW), _BF16),
        jax.ShapeDtypeStruct((B, S, 2 * D_MODEL), _BF16),
        jax.ShapeDtypeStruct((B, S, LANES), _BF16),
        jax.ShapeDtypeStruct((B, S, LANES), _BF16),
    )
    return pl.pallas_call(
        _inproj_kernel,
        grid=grid,
        in_specs=[
            row_blk(D),
            _resident(w_all.shape),
            pl.BlockSpec((TM_PROJ, LANES), lambda b, s: (s, 0)),
            pl.BlockSpec((TM_PROJ, LANES), lambda b, s: (s, 0)),
            _resident(bf.shape),
        ],
        out_specs=(row_blk(3 * FOX_W), row_blk(SWA_QW), row_blk(2 * SWA_KVW),
                   row_blk(2 * D_MODEL), row_blk(LANES), row_blk(LANES)),
        out_shape=out_shapes,
        scratch_shapes=[pltpu.VMEM((1, LANES), _F32)],
        compiler_params=pltpu.CompilerParams(
            dimension_semantics=("arbitrary", "arbitrary"), vmem_limit_bytes=VMEM_LIMIT),
        name="inproj",
    )(x, w_all, cos, sin, bf)


def _fox_kernel(q_ref, k_ref, v_ref, csq_ref, csk_ref, o_ref):
    for pp in range(PAIRS_PER_STEP):
        _fox_pair(pl.program_id(1) * PAIRS_PER_STEP + pp, slice(pp * LANES, (pp + 1) * LANES),
                  q_ref, k_ref, v_ref, csq_ref, csk_ref, o_ref)


def _fox_pair(hp, cols, q_ref, k_ref, v_ref, csq_ref, csk_ref, o_ref):
    S = q_ref.shape[1]
    tq = TQ_FOX

    pair_k = lax.broadcasted_iota(jnp.int32, (S, LANES), 1) // PAIR_LANES
    csk = csk_ref[0]
    kb = jnp.concatenate([k_ref[0, :, cols], jnp.where(pair_k == hp, csk, jnp.zeros_like(csk))], axis=1)
    v = jnp.concatenate([v_ref[0, :, cols], jnp.ones((S, LANES), _BF16)], axis=1)

    lane = lax.broadcasted_iota(jnp.int32, (tq, LANES), 1)
    pair, j, triple, _ = _side_lane_fields(lane)
    head_of = jnp.where(pair == hp, jnp.where(j < SIDE_LANES, triple & 1, -1), -1)
    keep0 = head_of == 0
    keep1 = head_of == 1
    causal = ((lax.broadcasted_iota(jnp.int32, (2 * tq, tq), 0) & (tq - 1))
              >= lax.broadcasted_iota(jnp.int32, (2 * tq, tq), 1))

    for i in reversed(range(S // tq)):
        rows = slice(i * tq, (i + 1) * tq)
        w = (i + 1) * tq
        q = q_ref[0, rows, cols]
        csq = csq_ref[0, rows, :]
        zq = jnp.zeros_like(q)
        qa = jnp.concatenate([
            jnp.concatenate([jnp.where(lane < HEAD_DIM, q, zq), jnp.where(keep0, csq, zq)], axis=1),
            jnp.concatenate([jnp.where(lane >= HEAD_DIM, q, zq), jnp.where(keep1, csq, zq)], axis=1),
        ], axis=0)
        s = _dot_nt(qa, kb[0:w, :])
        sd = jnp.where(causal, s[:, w - tq:w], -jnp.inf)
        m = jnp.max(sd, axis=1, keepdims=True)
        if i > 0:
            so = s[:, 0:w - tq]
            m = jnp.maximum(m, jnp.max(so, axis=1, keepdims=True))
            p = jnp.concatenate([jnp.exp2(so - m), jnp.exp2(sd - m)], axis=1)
        else:
            p = jnp.exp2(sd - m)
        ol = _dot(p.astype(_BF16), v[0:w, :])
        o = ol[:, 0:LANES] / ol[:, LANES:2 * LANES]
        o_ref[0, rows, cols] = jnp.where(lane < HEAD_DIM, o[0:tq], o[tq:2 * tq]).astype(_BF16)


def _fox_attention(fox_qkv, csq, csk):
    B, S, _ = fox_qkv.shape
    nstep = FOX_HEADS // 2 // PAIRS_PER_STEP
    wide = PAIRS_PER_STEP * LANES
    grid = (B, nstep)
    return pl.pallas_call(
        _fox_kernel,
        grid=grid,
        in_specs=[
            pl.BlockSpec((1, S, wide), lambda b, h: (b, 0, h)),
            pl.BlockSpec((1, S, wide), lambda b, h: (b, 0, nstep + h)),
            pl.BlockSpec((1, S, wide), lambda b, h: (b, 0, 2 * nstep + h)),
            pl.BlockSpec((1, S, LANES), lambda b, h: (b, 0, 0)),
            pl.BlockSpec((1, S, LANES), lambda b, h: (b, 0, 0)),
        ],
        out_specs=pl.BlockSpec((1, S, wide), lambda b, h: (b, 0, h)),
        out_shape=jax.ShapeDtypeStruct((B, S, FOX_W), _BF16),
        compiler_params=pltpu.CompilerParams(
            dimension_semantics=("arbitrary", "arbitrary"), vmem_limit_bytes=VMEM_LIMIT),
        name="fox_attn",
    )(fox_qkv, fox_qkv, fox_qkv, csq, csk)


def _swa_kernel(sink_ref, q_ref, kv_ref, o_ref):
    i = pl.program_id(1)
    ts = q_ref.shape[1]
    W = WINDOW
    lane = lax.broadcasted_iota(jnp.int32, (W, LANES), 1)
    nrow = SWA_GROUP * W
    qrow = lax.broadcasted_iota(jnp.int32, (nrow, 2 * W), 0) & (W - 1)
    kcol = lax.broadcasted_iota(jnp.int32, (nrow, 2 * W), 1)
    back = W + qrow - kcol
    band = jnp.where(back >= 0, jnp.where(back < WINDOW, 0.0, -jnp.inf), -jnp.inf)
    band_first = jnp.where(kcol >= W, band, -jnp.inf)

    for w in range(ts // W):
        qstart = pl.multiple_of(i * ts + w * W, W)
        pstart = pl.multiple_of(jnp.maximum(qstart - W, 0), W)
        kvw = jnp.concatenate([kv_ref[0, pl.ds(pstart, W), :], kv_ref[0, pl.ds(qstart, W), :]], axis=0)
        kk = kvw[:, 0:LANES]
        vv = jnp.concatenate([kvw[:, LANES:2 * LANES], jnp.ones((2 * W, LANES), _BF16)], axis=1)
        bias = jnp.where(i == 0, band_first, band) if w == 0 else band

        og = []
        for g in range(SWA_KV_HEADS):
            in_head = (lane < HEAD_DIM) if g == 0 else (lane >= HEAD_DIM)
            rows = []
            for r in range(SWA_GROUP):
                qb = q_ref[0, w * W:(w + 1) * W, r * LANES:(r + 1) * LANES]
                rows.append(jnp.where(in_head, qb, jnp.zeros_like(qb)))
            ql = jnp.concatenate(rows, axis=0)
            s = _dot_nt(ql, kk) + bias
            sink = jnp.concatenate([jnp.full((W, LANES), sink_ref[g * SWA_GROUP + r] * LOG2E, _F32)
                                    for r in range(SWA_GROUP)], axis=0)
            m = jnp.max(s, axis=1, keepdims=True)
            e = jnp.exp2(s - m)
            oe = _dot(e.astype(_BF16), vv)
            den = oe[:, LANES:2 * LANES] + jnp.exp2(sink - m)
            og.append(oe[:, 0:LANES] / den)

        for r in range(SWA_GROUP):
            o = jnp.where(lane < HEAD_DIM, og[0][r * W:(r + 1) * W], og[1][r * W:(r + 1) * W])
            o_ref[0, w * W:(w + 1) * W, r * LANES:(r + 1) * LANES] = o.astype(_BF16)


def _swa_attention(sinks, swa_q, swa_kv):
    B, S, _ = swa_q.shape
    grid = (B, S // TS_SWA)
    return pl.pallas_call(
        _swa_kernel,
        grid=grid,
        in_specs=[
            pl.BlockSpec(memory_space=pltpu.SMEM),
            pl.BlockSpec((1, TS_SWA, SWA_QW), lambda b, i: (b, i, 0)),
            pl.BlockSpec((1, S, 2 * SWA_KVW), lambda b, i: (b, 0, 0)),
        ],
        out_specs=pl.BlockSpec((1, TS_SWA, SWA_QW), lambda b, i: (b, i, 0)),
        out_shape=jax.ShapeDtypeStruct((B, S, SWA_QW), _BF16),
        compiler_params=pltpu.CompilerParams(
            dimension_semantics=("arbitrary", "arbitrary"), vmem_limit_bytes=VMEM_LIMIT),
        name="swa_attn",
    )(sinks, swa_q, swa_kv)


def _layer_norm(t, g, b):
    mu = jnp.mean(t, axis=-1, keepdims=True)
    d = t - mu
    var = jnp.mean(d * d, axis=-1, keepdims=True)
    return d * lax.rsqrt(var + LN_EPS) * g + b


def _out_kernel(x_ref, of_ref, os_ref, gate_ref, wof_ref, wos_ref, wout_ref, g1_ref, b1_ref,
                wg_ref, wu_ref, wd_ref, g2_ref, b2_ref, o_ref, pre1_ref, pre2_ref):
    @pl.when(pl.program_id(0) == 0)
    def _():
        pre1_ref[...] = jnp.zeros_like(pre1_ref)
        pre2_ref[...] = jnp.zeros_like(pre2_ref)

    of_p = _dot(of_ref[...], wof_ref[...])
    os_p = _dot(os_ref[...], wos_ref[...])

    h1 = _layer_norm(pre1_ref[...], g1_ref[...], b1_ref[...])
    lhs = h1.astype(_BF16)

    def hidden(c, tie=None):
        cols = slice(FF_BOUNDS[c], FF_BOUNDS[c + 1])
        g = _dot(lhs, wg_ref[:, cols])
        u = _dot(lhs, wu_ref[:, cols])
        if tie is not None:
            u = jnp.concatenate([u[:, 0:D_MODEL] + tie, u[:, D_MODEL:]], axis=1)
        return (g * (1.0 / (1.0 + jnp.exp(-g))) * u).astype(_BF16)

    hc = [hidden(0)]
    merged = (gate_ref[:, 0:D_MODEL].astype(_F32) * of_p
              + gate_ref[:, D_MODEL:2 * D_MODEL].astype(_F32) * os_p)
    mix = _dot(merged.astype(_BF16), wout_ref[...])
    pre1_ref[...] = DEEPNORM_ALPHA * x_ref[...] + mix

    out = _layer_norm(pre2_ref[...], g2_ref[...], b2_ref[...])
    o_ref[...] = out
    zero = pltpu.bitcast(lax.shift_right_logical(lax.shift_right_logical(
        pltpu.bitcast(out, jnp.uint32), jnp.uint32(16)), jnp.uint32(16)), _F32)
    assert FF_BOUNDS[-1] - FF_BOUNDS[-2] >= D_MODEL

    last = len(FF_BOUNDS) - 2
    hc += [hidden(c, zero if c == last else None) for c in range(1, last + 1)]
    ff = _dot(jnp.concatenate(hc, axis=1), wd_ref[...])
    pre2_ref[...] = DEEPNORM_ALPHA * h1 + ff


def _out_ffn(x2, o_f, o_s, gates, wof, wos, wout, g1, b1, wg, wu, wd, g2, b2):
    T, D = x2.shape
    nblk = T // TM_OUT
    grid = (nblk + 2,)
    in_blk = lambda w: pl.BlockSpec((TM_OUT, w), lambda t: (jnp.minimum(t, nblk - 1), 0))
    return pl.pallas_call(
        _out_kernel,
        grid=grid,
        in_specs=[in_blk(D), in_blk(FOX_W), in_blk(SWA_QW), in_blk(2 * D_MODEL),
                  _resident(wof.shape), _resident(wos.shape), _resident(wout.shape),
                  _resident(g1.shape), _resident(b1.shape),
                  _resident(wg.shape), _resident(wu.shape), _resident(wd.shape),
                  _resident(g2.shape), _resident(b2.shape)],
        out_specs=pl.BlockSpec((TM_OUT, D), lambda t: (jnp.maximum(t - 2, 0), 0)),
        out_shape=jax.ShapeDtypeStruct((T, D), _F32),
        scratch_shapes=[pltpu.VMEM((TM_OUT, D), _F32), pltpu.VMEM((TM_OUT, D), _F32)],
        compiler_params=pltpu.CompilerParams(
            dimension_semantics=("arbitrary",), vmem_limit_bytes=VMEM_LIMIT),
        name="out_ffn",
    )(x2, o_f, o_s, gates, wof, wos, wout, g1, b1, wg, wu, wd, g2, b2)


def _pair_swa_heads(t, axis):
    shp = t.shape
    t = t.reshape(shp[:axis] + (SWA_KV_HEADS, SWA_GROUP, HEAD_DIM) + shp[axis + 1:])
    return jnp.swapaxes(t, axis, axis + 1).reshape(shp)


def _side_lane_placement():
    place = np.zeros((FOX_HEADS, LANES), np.float32)
    for lane in range(LANES):
        pair, j = divmod(lane, PAIR_LANES)
        if j < SIDE_LANES:
            place[2 * pair + (j // 3) % 2, lane] = 1.0
    return jnp.asarray(place)


def _rope_tables(S):
    half = HEAD_DIM // 2
    inv_freq = ROPE_THETA ** (-jnp.arange(0, half, dtype=_F32) / half)
    ang = jnp.arange(S, dtype=_F32)[:, None] * inv_freq[None, :]
    cos = jnp.tile(jnp.cos(ang), (1, LANES // half))
    sin = jnp.sin(ang)
    sin = jnp.tile(jnp.concatenate([-sin, sin], axis=1), (1, LANES // HEAD_DIM))
    return cos, sin


def kernel(x, w_in, b_forget, attn_sinks, w_o_fox, w_o_swa, w_out, ln1_g, ln1_b,
           w_gate, w_up, w_down, ln2_g, ln2_b):
    B, S, D = x.shape
    assert DEPTH == 1 and w_in.shape[0] == 1
    w = w_in[0]
    sizes = [FOX_W, FOX_W, FOX_W, FOX_HEADS, SWA_QW, SWA_KVW, SWA_KVW, D_MODEL, D_MODEL]
    offs = np.concatenate([[0], np.cumsum(sizes)])
    wq_f, wk_f, wv_f, w_fg, wq_s, wk_s, wv_s, wg_f, wg_s = [w[:, offs[n]:offs[n + 1]] for n in range(9)]

    place = _side_lane_placement()
    w_fg3 = jnp.dot(w_fg, place, precision=lax.Precision.HIGHEST)
    w_all = jnp.concatenate(
        [wq_f * (QK_SCALE * LOG2E), wk_f, wv_f, _pair_swa_heads(wq_s, 1) * (QK_SCALE * LOG2E), wk_s, wv_s, wg_f, wg_s, w_fg3],
        axis=1).astype(_BF16)
    bf = jnp.dot(b_forget[0][None, :], place, precision=lax.Precision.HIGHEST)
    cos, sin = _rope_tables(S)

    fox_qkv, swa_q, swa_kv, gates, csq, csk = _inproj(x, w_all, cos, sin, bf)
    o_f = _fox_attention(fox_qkv, csq, csk)
    o_s = _swa_attention(attn_sinks[0], swa_q, swa_kv)

    T = B * S
    row = lambda v: v[0][None, :]
    out = _out_ffn(
        x.reshape(T, D), o_f.reshape(T, FOX_W), o_s.reshape(T, SWA_QW), gates.reshape(T, 2 * D_MODEL),
        w_o_fox[0].astype(_BF16), _pair_swa_heads(w_o_swa[0], 0).astype(_BF16), w_out[0].astype(_BF16),
        row(ln1_g), row(ln1_b),
        w_gate[0].astype(_BF16), w_up[0].astype(_BF16), w_down[0].astype(_BF16),
        row(ln2_g), row(ln2_b))
    return out.reshape(B, S, D)
```
